```python
import jax, jax.numpy as jnp
from jax import lax
import numpy as np

D_MODEL = 1024
BATCH = 8
SEQ = 2048
DEPTH = 2
DEC_BATCH = 8
DEC_SEQ = 16
PAST_LEN = 1024

CHUNK = 64
N_MIXERS = 2
N_FOX_LAYERS = (DEPTH + 1) // 2
N_GDN_LAYERS = DEPTH // 2
FOX_HEADS = 16
FOX_HEAD_DIM = D_MODEL // FOX_HEADS
Q_BLOCK = 128
GDN_HEADS = 8
GDN_HEAD_DIM = D_MODEL // GDN_HEADS
GDN_CONV = 4
GDN_CONV_DIM = 3 * GDN_HEADS * GDN_HEAD_DIM
GDN_CHUNK = CHUNK
D_FF = 7 * D_MODEL // 2
N_EXPERTS = 8
TOP_K = 2
EPS = 1e-6

kernel_name = 'fox_gdn_interleaved_stream_step'


def rms_norm(x, gain):
    xf = x.astype(jnp.float32)
    y = xf * lax.rsqrt(jnp.mean(xf * xf, axis=-1, keepdims=True) + EPS)
    return (y * gain.astype(jnp.float32)).astype(x.dtype)


def l2_norm(x):
    return x * lax.rsqrt(jnp.sum(x * x, axis=-1, keepdims=True) + EPS)


def swiglu(h, w1, w3, w2):
    return (jax.nn.silu(h @ w1) * (h @ w3)) @ w2


def dense_ffn(x, norm_g, w1, w3, w2):
    return x + swiglu(rms_norm(x, norm_g), w1, w3, w2)


def moe_ffn(x, norm_g, router, w1, w3, w2):
    b, t, d = x.shape
    h = rms_norm(x, norm_g).reshape(b * t, d)
    logits = (h @ router).astype(jnp.float32)
    top_v, top_i = lax.top_k(logits, TOP_K)
    gates = jax.nn.softmax(top_v, axis=-1)
    comb = jnp.sum(jax.nn.one_hot(top_i, N_EXPERTS, dtype=jnp.float32) * gates[..., None], axis=1)
    out = jnp.zeros((b * t, d), jnp.float32)
    for e in range(N_EXPERTS):
        out = out + comb[:, e:e + 1] * swiglu(h, w1[e], w3[e], w2[e]).astype(jnp.float32)
    return x + out.reshape(b, t, d).astype(x.dtype)


def fox_project(x, norm_g, w_qkv, q_gain, k_gain, w_f, b_f):
    b, t, _ = x.shape
    h = rms_norm(x, norm_g)
    qkv = (h @ w_qkv).reshape(b, t, 3, FOX_HEADS, FOX_HEAD_DIM)
    q = rms_norm(qkv[:, :, 0], q_gain)
    k = rms_norm(qkv[:, :, 1], k_gain)
    v = qkv[:, :, 2]
    logf = jax.nn.log_sigmoid((h @ w_f).astype(jnp.float32) + b_f.astype(jnp.float32))
    return q, k, v, logf


def fox_attend(q, c_q, pos_q, k, v, c_k, pos_k):
    s = jnp.einsum('bqhd,bkhd->bhqk', q, k, preferred_element_type=jnp.float32) * (FOX_HEAD_DIM ** -0.5)
    s = s + (jnp.transpose(c_q, (0, 2, 1))[..., :, None] - jnp.transpose(c_k, (0, 2, 1))[..., None, :])
    s = jnp.where(pos_k[None, None, None, :] <= pos_q[None, None, :, None], s, -jnp.inf)
    p = jax.nn.softmax(s, axis=-1)
    return jnp.einsum('bhqk,bkhd->bqhd', p.astype(v.dtype), v)


def fox_layer_prompt(x, norm_g, w_qkv, q_gain, k_gain, w_f, b_f, w_o):
    b, t, _ = x.shape
    q, k, v, logf = fox_project(x, norm_g, w_qkv, q_gain, k_gain, w_f, b_f)
    c = jnp.cumsum(logf, axis=1)
    pos = jnp.arange(t)

    def block(i):
        start = i * Q_BLOCK
        qb = lax.dynamic_slice_in_dim(q, start, Q_BLOCK, axis=1)
        cb = lax.dynamic_slice_in_dim(c, start, Q_BLOCK, axis=1)
        return fox_attend(qb, cb, start + jnp.arange(Q_BLOCK), k, v, c, pos)

    o = lax.map(block, jnp.arange(t // Q_BLOCK))
    o = jnp.moveaxis(o, 0, 1).reshape(b, t, FOX_HEADS * FOX_HEAD_DIM)
    return x + o @ w_o, k, v, logf


def fox_layer_sample(x, cache_k, cache_v, cache_logf, norm_g, w_qkv, q_gain, k_gain, w_f, b_f, w_o):
    b, t, _ = x.shape
    past = cache_k.shape[1]
    q, k, v, logf = fox_project(x, norm_g, w_qkv, q_gain, k_gain, w_f, b_f)
    k_all = jnp.concatenate([cache_k.astype(k.dtype), k], axis=1)
    v_all = jnp.concatenate([cache_v.astype(v.dtype), v], axis=1)
    c_all = jnp.cumsum(jnp.concatenate([cache_logf.astype(jnp.float32), logf], axis=1), axis=1)
    pos = jnp.arange(past + t)
    o = fox_attend(q, c_all[:, past:], pos[past:], k_all, v_all, c_all, pos)
    o = o.reshape(b, t, FOX_HEADS * FOX_HEAD_DIM)
    return x + o @ w_o, k, v, logf


def gated_delta_chunked(q, k, v, beta, g, s0, chunk):
    b, t, nh, dk = q.shape
    dv = v.shape[-1]
    n = t // chunk

    def to_chunks(a):
        a = a.reshape((b, n, chunk, nh) + a.shape[3:])
        return jnp.moveaxis(a, 3, 1)

    q, k, v, beta, g = (to_chunks(a) for a in (q, k, v, beta, g))
    gc = jnp.cumsum(g, axis=-1)
    idx = jnp.arange(chunk)
    causal = idx[:, None] >= idx[None, :]
    strict = idx[:, None] > idx[None, :]
    decay = jnp.exp(jnp.where(causal, gc[..., :, None] - gc[..., None, :], -jnp.inf))
    kk = jnp.einsum('bhnid,bhnjd->bhnij', k, k)
    a_mat = jnp.where(strict, beta[..., :, None] * kk * decay, 0.0) + jnp.eye(chunk, dtype=jnp.float32)
    rhs = jnp.concatenate([v * beta[..., None], k * (beta * jnp.exp(gc))[..., None]], axis=-1)
    sol = lax.linalg.triangular_solve(a_mat, rhs, left_side=True, lower=True)
    u, w = sol[..., :dv], sol[..., dv:]
    attn = jnp.einsum('bhnid,bhnjd->bhnij', q, k) * decay
    q_dec = q * jnp.exp(gc)[..., None]
    g_last = gc[..., -1]
    k_dec = k * jnp.exp(g_last[..., None] - gc)[..., None]

    def step(s, xs):
        u_c, w_c, attn_c, q_c, k_c, gl = xs
        v_new = u_c - jnp.einsum('bhcd,bhde->bhce', w_c, s)
        o_c = jnp.einsum('bhcd,bhde->bhce', q_c, s) + jnp.einsum('bhij,bhje->bhie', attn_c, v_new)
        s = s * jnp.exp(gl)[..., None, None] + jnp.einsum('bhcd,bhce->bhde', k_c, v_new)
        return s, o_c

    xs = tuple(jnp.moveaxis(a, 2, 0) for a in (u, w, attn, q_dec, k_dec, g_last))
    s_final, o = lax.scan(step, s0, xs)
    o = jnp.transpose(o, (1, 0, 3, 2, 4)).reshape(b, t, nh, dv)
    return o, s_final


def gdn_layer(x, conv_hist, s0, chunk, norm_g, w_qkv, conv_w, w_beta, w_a, a_log, dt_bias, w_gate, out_gain, w_o):
    b, t, _ = x.shape
    h = rms_norm(x, norm_g)
    raw = h @ w_qkv
    padded = jnp.concatenate([conv_hist.astype(raw.dtype), raw], axis=1)
    mixed = lax.conv_general_dilated(padded, conv_w[:, None, :].astype(raw.dtype), window_strides=(1,),
                                     padding='VALID', dimension_numbers=('NWC', 'WIO', 'NWC'),
                                     feature_group_count=GDN_CONV_DIM)
    mixed = jax.nn.silu(mixed).astype(jnp.float32).reshape(b, t, 3, GDN_HEADS, GDN_HEAD_DIM)
    q = l2_norm(mixed[:, :, 0]) * (GDN_HEAD_DIM ** -0.5)
    k = l2_norm(mixed[:, :, 1])
    v = mixed[:, :, 2]
    beta = jax.nn.sigmoid((h @ w_beta).astype(jnp.float32))
    g = -jnp.exp(a_log.astype(jnp.float32)) * jax.nn.softplus((h @ w_a).astype(jnp.float32) + dt_bias.astype(jnp.float32))
    o, s_new = gated_delta_chunked(q, k, v, beta, g, s0.astype(jnp.float32), chunk)
    gate = jax.nn.silu((h @ w_gate).astype(jnp.float32)).reshape(b, t, GDN_HEADS, GDN_HEAD_DIM)
    o = (rms_norm(o, out_gain) * gate).reshape(b, t, GDN_HEADS * GDN_HEAD_DIM).astype(x.dtype)
    new_hist = padded[:, padded.shape[1] - (GDN_CONV - 1):]
    return x + o @ w_o, s_new.astype(s0.dtype), new_hist


def setup_inputs(seed: int = 0) -> dict:
    key = jax.random.key(seed)
    keys = iter(jax.random.split(key, 48))
    f32 = jnp.float32

    def normal(shape, scale):
        return jax.random.normal(next(keys), shape, f32) * scale

    def gain(shape):
        return 1.0 + 0.1 * jax.random.normal(next(keys), shape, f32)

    na, nb, d = N_FOX_LAYERS, N_GDN_LAYERS, D_MODEL
    fw = FOX_HEADS * FOX_HEAD_DIM
    gw = GDN_HEADS * GDN_HEAD_DIM
    inp = {}
    inp['x_prompt'] = normal((BATCH, SEQ, d), 1.0)
    inp['x_sample'] = normal((DEC_BATCH, DEC_SEQ, d), 1.0)
    inp['cache_fox_k'] = normal((na, DEC_BATCH, PAST_LEN, FOX_HEADS, FOX_HEAD_DIM), 1.0)
    inp['cache_fox_v'] = normal((na, DEC_BATCH, PAST_LEN, FOX_HEADS, FOX_HEAD_DIM), 1.0)
    inp['cache_fox_logf'] = jax.nn.log_sigmoid(3.0 + normal((na, DEC_BATCH, PAST_LEN, FOX_HEADS), 0.5))
    inp['state_gdn_s'] = normal((nb, DEC_BATCH, GDN_HEADS, GDN_HEAD_DIM, GDN_HEAD_DIM), 0.3)
    inp['state_gdn_conv'] = normal((nb, DEC_BATCH, GDN_CONV - 1, GDN_CONV_DIM), 1.0)
    inp['fox_norm'] = gain((na, d))
    inp['fox_w_qkv'] = normal((na, d, 3 * fw), d ** -0.5)
    inp['fox_q_gain'] = gain((na, FOX_HEAD_DIM))
    inp['fox_k_gain'] = gain((na, FOX_HEAD_DIM))
    inp['fox_w_f'] = normal((na, d, FOX_HEADS), d ** -0.5)
    inp['fox_b_f'] = 3.0 + normal((na, FOX_HEADS), 0.5)
    inp['fox_w_o'] = normal((na, fw, d), fw ** -0.5)
    inp['ffn_norm'] = gain((na, d))
    inp['ffn_w1'] = normal((na, d, D_FF), d ** -0.5)
    inp['ffn_w3'] = normal((na, d, D_FF), d ** -0.5)
    inp['ffn_w2'] = normal((na, D_FF, d), D_FF ** -0.5)
    inp['gdn_norm'] = gain((nb, d))
    inp['gdn_w_qkv'] = normal((nb, d, GDN_CONV_DIM), d ** -0.5)
    inp['gdn_conv_w'] = normal((nb, GDN_CONV, GDN_CONV_DIM), GDN_CONV ** -0.5)
    inp['gdn_w_beta'] = normal((nb, d, GDN_HEADS), d ** -0.5)
    inp['gdn_w_a'] = normal((nb, d, GDN_HEADS), d ** -0.5)
    inp['gdn_a_log'] = jnp.log(jax.random.uniform(next(keys), (nb, GDN_HEADS), f32, 1.0, 16.0))
    dt = jnp.exp(jax.random.uniform(next(keys), (nb, GDN_HEADS), f32, float(np.log(1e-3)), float(np.log(1e-1))))
    inp['gdn_dt_bias'] = jnp.log(jnp.expm1(dt))
    inp['gdn_w_gate'] = normal((nb, d, gw), d ** -0.5)
    inp['gdn_out_gain'] = gain((nb, GDN_HEAD_DIM))
    inp['gdn_w_o'] = normal((nb, gw, d), gw ** -0.5)
    inp['moe_norm'] = gain((nb, d))
    inp['moe_router'] = normal((nb, d, N_EXPERTS), d ** -0.5)
    inp['moe_w1'] = normal((nb, N_EXPERTS, d, D_FF), d ** -0.5)
    inp['moe_w3'] = normal((nb, N_EXPERTS, d, D_FF), d ** -0.5)
    inp['moe_w2'] = normal((nb, N_EXPERTS, D_FF, d), D_FF ** -0.5)
    return inp


def reference(x_prompt, x_sample, cache_fox_k, cache_fox_v, cache_fox_logf, state_gdn_s, state_gdn_conv,
              fox_norm, fox_w_qkv, fox_q_gain, fox_k_gain, fox_w_f, fox_b_f, fox_w_o,
              ffn_norm, ffn_w1, ffn_w3, ffn_w2,
              gdn_norm, gdn_w_qkv, gdn_conv_w, gdn_w_beta, gdn_w_a, gdn_a_log, gdn_dt_bias,
              gdn_w_gate, gdn_out_gain, gdn_w_o,
              moe_norm, moe_router, moe_w1, moe_w3, moe_w2):
    yp, ys = x_prompt, x_sample
    bp = x_prompt.shape[0]
    kp_l, vp_l, lp_l, ks_l, vs_l, ls_l = [], [], [], [], [], []
    sp_l, cp_l, ss_l, cs_l = [], [], [], []
    for i in range(DEPTH):
        j = i // N_MIXERS
        if i % N_MIXERS == 0:
            fox_w = (fox_norm[j], fox_w_qkv[j], fox_q_gain[j], fox_k_gain[j], fox_w_f[j], fox_b_f[j], fox_w_o[j])
            yp, kp, vp, lp = fox_layer_prompt(yp, *fox_w)
            ys, ks, vs, ls = fox_layer_sample(ys, cache_fox_k[j], cache_fox_v[j], cache_fox_logf[j], *fox_w)
            kp_l.append(kp); vp_l.append(vp); lp_l.append(lp)
            ks_l.append(ks); vs_l.append(vs); ls_l.append(ls)
            yp = dense_ffn(yp, ffn_norm[j], ffn_w1[j], ffn_w3[j], ffn_w2[j])
            ys = dense_ffn(ys, ffn_norm[j], ffn_w1[j], ffn_w3[j], ffn_w2[j])
        else:
            gdn_w = (gdn_norm[j], gdn_w_qkv[j], gdn_conv_w[j], gdn_w_beta[j], gdn_w_a[j], gdn_a_log[j],
                     gdn_dt_bias[j], gdn_w_gate[j], gdn_out_gain[j], gdn_w_o[j])
            hist0 = jnp.zeros((bp, GDN_CONV - 1, GDN_CONV_DIM), yp.dtype)
            s_zero = jnp.zeros((bp, GDN_HEADS, GDN_HEAD_DIM, GDN_HEAD_DIM), yp.dtype)
            yp, sp, cp = gdn_layer(yp, hist0, s_zero, GDN_CHUNK, *gdn_w)
            ys, ss, cs = gdn_layer(ys, state_gdn_conv[j], state_gdn_s[j], ys.shape[1], *gdn_w)
            sp_l.append(sp); cp_l.append(cp); ss_l.append(ss); cs_l.append(cs)
            yp = moe_ffn(yp, moe_norm[j], moe_router[j], moe_w1[j], moe_w3[j], moe_w2[j])
            ys = moe_ffn(ys, moe_norm[j], moe_router[j], moe_w1[j], moe_w3[j], moe_w2[j])
    fox_k_prompt = jnp.stack(kp_l)
    fox_v_prompt = jnp.stack(vp_l)
    fox_logf_prompt = jnp.stack(lp_l)
    fox_k_sample = jnp.stack(ks_l)
    fox_v_sample = jnp.stack(vs_l)
    fox_logf_sample = jnp.stack(ls_l)
    gdn_s_prompt = jnp.stack(sp_l)
    gdn_conv_prompt = jnp.stack(cp_l)
    gdn_s_sample = jnp.stack(ss_l)
    gdn_conv_sample = jnp.stack(cs_l)
    return (yp, ys, fox_k_prompt, fox_v_prompt, fox_logf_prompt, fox_k_sample, fox_v_sample, fox_logf_sample,
            gdn_s_prompt, gdn_conv_prompt, gdn_s_sample, gdn_conv_sample)
```

```python
import functools

import jax
import jax.numpy as jnp
from jax import lax
from jax.experimental import pallas as pl
from jax.experimental.pallas import tpu as pltpu

F32 = jnp.float32
BF16 = jnp.bfloat16
EPS = 1e-6
LANES = 128
VMEM_LIMIT = 56 * 1024 * 1024

FOX_HEADS = 16
FOX_DH = 64
GDN_HEADS = 8
GDN_DH = 128
GDN_CONV = 4
N_EXPERTS = 8
HIGHEST = lax.Precision.HIGHEST


def _params(*sem):
    return pltpu.CompilerParams(dimension_semantics=sem, vmem_limit_bytes=VMEM_LIMIT)


def _rms(x, gain):
    return x * lax.rsqrt(jnp.mean(x * x, axis=-1, keepdims=True) + EPS) * gain


def _silu(x):
    return x * (1.0 / (1.0 + jnp.exp(-x)))


def _softplus(x):
    return jnp.maximum(x, 0.0) + jnp.log(1.0 + jnp.exp(-jnp.abs(x)))


def _split3(c):
    hi = c.astype(BF16).astype(F32)
    r = c - hi
    mid = r.astype(BF16).astype(F32)
    lo = (r - mid).astype(BF16).astype(F32)
    return hi, mid, lo


def _fox_proj_kernel(x_ref, g_ref, w_ref, qg_ref, kg_ref, bf_ref, gm_ref, gmt_ref,
                     q_ref, k_ref, v_ref, lf_ref):
    d = x_ref.shape[1]
    h = _rms(x_ref[...], g_ref[...])
    y = jnp.dot(h.astype(BF16), w_ref[...], preferred_element_type=F32)

    def headnorm(z, gain):
        ss = jnp.dot((z * z).astype(BF16), gm_ref[...], preferred_element_type=F32)
        inv = lax.rsqrt(ss * (1.0 / FOX_DH) + EPS)
        hi = inv.astype(BF16)
        lo = (inv - hi.astype(F32)).astype(BF16)
        full = (jnp.dot(hi, gmt_ref[...], preferred_element_type=F32)
                + jnp.dot(lo, gmt_ref[...], preferred_element_type=F32))
        return z * full * gain

    q_ref[...] = (headnorm(y[:, :d], qg_ref[...]) * (FOX_DH ** -0.5)).astype(q_ref.dtype)
    k_ref[...] = headnorm(y[:, d:2 * d], kg_ref[...])
    v_ref[...] = y[:, 2 * d:3 * d]
    z = y[:, 3 * d:3 * d + LANES] + bf_ref[...]
    lf = jnp.minimum(z, 0.0) - jnp.log(1.0 + jnp.exp(-jnp.abs(z)))
    lf_ref[...] = lf[:, :FOX_HEADS]


def _fox_proj(x, norm_g, w_qkv, q_gain, k_gain, w_f, b_f, tm):
    n, d = x.shape
    w_all = jnp.concatenate(
        [w_qkv, w_f, jnp.zeros((d, LANES - FOX_HEADS), F32)], axis=1).astype(BF16)
    bf = jnp.concatenate([b_f, jnp.zeros((LANES - FOX_HEADS,), F32)])[None, :]
    head_of_col = jnp.arange(d) // FOX_DH
    gm = (head_of_col[:, None] == jnp.arange(LANES)[None, :]).astype(BF16)
    gmt = gm.T
    qg = jnp.tile(q_gain, FOX_HEADS)[None, :]
    kg = jnp.tile(k_gain, FOX_HEADS)[None, :]
    const = lambda i: (0, 0)
    row = lambda i: (i, 0)
    return pl.pallas_call(
        _fox_proj_kernel,
        grid=(n // tm,),
        in_specs=[pl.BlockSpec((tm, d), row), pl.BlockSpec((1, d), const),
                  pl.BlockSpec((d, 3 * d + LANES), const), pl.BlockSpec((1, d), const),
                  pl.BlockSpec((1, d), const), pl.BlockSpec((1, LANES), const),
                  pl.BlockSpec((d, LANES), const), pl.BlockSpec((LANES, d), const)],
        out_specs=[pl.BlockSpec((tm, d), row), pl.BlockSpec((tm, d), row), pl.BlockSpec((tm, d), row),
                   pl.BlockSpec((tm, FOX_HEADS), row)],
        out_shape=[jax.ShapeDtypeStruct((n, d), BF16), jax.ShapeDtypeStruct((n, d), F32),
                   jax.ShapeDtypeStruct((n, d), F32), jax.ShapeDtypeStruct((n, FOX_HEADS), F32)],
        compiler_params=_params("parallel"),
        name="fox_proj",
    )(x, norm_g[None, :], w_all, qg, kg, bf, gm, gmt)


def _cumsum_kernel(x_ref, o_ref):
    x = x_ref[0]
    t = x.shape[0]
    row = lax.broadcasted_iota(jnp.int32, x.shape, 0)
    s = 1
    while s < t:
        x = x + jnp.where(row >= s, pltpu.roll(x, s, axis=0), 0.0)
        s *= 2
    o_ref[0] = x


def _cumsum_time(x):
    b, t, hn = x.shape
    spec = pl.BlockSpec((1, t, hn), lambda i: (i, 0, 0))
    return pl.pallas_call(
        _cumsum_kernel, grid=(b,), in_specs=[spec], out_specs=spec,
        out_shape=jax.ShapeDtypeStruct(x.shape, F32),
        compiler_params=_params("parallel"), name="logf_cumsum",
    )(x)


def _head_col(c16, head):
    lane = lax.broadcasted_iota(jnp.int32, c16.shape, 1)
    return jnp.sum(jnp.where(lane == head, c16, 0.0), axis=1, keepdims=True)


def _aug(data, ccol, hh, is_q):
    lane = lax.broadcasted_iota(jnp.int32, data.shape, 1)
    base = FOX_DH if hh == 0 else 0
    in_data = (lane < FOX_DH) if hh == 0 else (lane >= FOX_DH)
    hi, mid, lo = _split3(ccol)
    if is_q:
        pieces = (hi, mid, lo, 1.0, 1.0, 1.0)
    else:
        pieces = (1.0, 1.0, 1.0, -hi, -mid, -lo)
    aug = jnp.zeros(data.shape, F32)
    for i, p in enumerate(pieces):
        aug = jnp.where(lane == base + i, p, aug)
    return jnp.where(in_data, data, aug)


def _v_aug(v2, hh):
    lane = lax.broadcasted_iota(jnp.int32, v2.shape, 1)
    in_data = (lane < FOX_DH) if hh == 0 else (lane >= FOX_DH)
    return jnp.where(in_data, v2, 1.0)


def _finish_heads(acc0, acc1):
    lane = lax.broadcasted_iota(jnp.int32, acc0.shape, 1)
    o0 = acc0 / pltpu.roll(acc0, FOX_DH, axis=1)
    o1 = acc1 / pltpu.roll(acc1, FOX_DH, axis=1)
    return jnp.where(lane < FOX_DH, o0, o1)


def _fox_attn_kernel(q_ref, k_ref, v_ref, c_ref, o_ref, kt_s, v_s, *, tq, tk):
    hp = pl.program_id(1)
    qi = pl.program_id(2)
    t = k_ref.shape[0]
    nkb = t // tk
    nd = tq // tk

    @pl.when(qi == 0)
    def _():
        c16 = c_ref[0]
        k2 = k_ref[...]
        v2 = v_ref[...]
        for hh in range(2):
            kaug = _aug(k2, _head_col(c16, 2 * hp + hh), hh, False)
            for jb in range(nkb):
                kt_s[hh, jb] = kaug[jb * tk:(jb + 1) * tk, :].T.astype(BF16)
            v_s[hh] = _v_aug(v2, hh).astype(BF16)

    q2 = q_ref[...].astype(F32)
    cq16 = c_ref[0, pl.ds(pl.multiple_of(qi * tq, tq), tq), :]
    rows = lax.broadcasted_iota(jnp.int32, (tq, tk), 0)
    cols = lax.broadcasted_iota(jnp.int32, (tq, tk), 1)
    accs = []
    for hh in range(2):
        qaug = _aug(q2, _head_col(cq16, 2 * hp + hh), hh, True).astype(BF16)

        def block(j, m, acc, mask):
            s = jnp.dot(qaug, kt_s[hh, j], preferred_element_type=F32)
            if mask is not None:
                s = jnp.where(mask, s, -jnp.inf)
            m_new = jnp.maximum(m, jnp.max(s, axis=1, keepdims=True))
            p = jnp.exp(s - m_new).astype(BF16)
            vblk = v_s[hh, pl.ds(pl.multiple_of(j * tk, tk), tk), :]
            acc = jnp.exp(m - m_new) * acc + jnp.dot(p, vblk, preferred_element_type=F32)
            return m_new, acc

        m = jnp.full((tq, 1), -jnp.inf, F32)
        acc = jnp.zeros((tq, LANES), F32)
        for dblk in range(nd):
            m, acc = block(qi * nd + dblk, m, acc, cols + dblk * tk <= rows)
        m, acc = lax.fori_loop(0, qi * nd, lambda j, c: block(j, c[0], c[1], None), (m, acc))
        accs.append(acc)
    o_ref[...] = _finish_heads(accs[0], accs[1]).astype(o_ref.dtype)


def _fox_attn_prompt(q, k, v, c, b, t, tq, tk):
    n, d = q.shape
    nq = t // tq
    nhp = d // LANES
    return pl.pallas_call(
        functools.partial(_fox_attn_kernel, tq=tq, tk=tk),
        grid=(b, nhp, nq),
        in_specs=[pl.BlockSpec((tq, LANES), lambda bi, hp, qi: (bi * nq + qi, hp)),
                  pl.BlockSpec((t, LANES), lambda bi, hp, qi: (bi, hp)),
                  pl.BlockSpec((t, LANES), lambda bi, hp, qi: (bi, hp)),
                  pl.BlockSpec((1, t, FOX_HEADS), lambda bi, hp, qi: (bi, 0, 0))],
        out_specs=pl.BlockSpec((tq, LANES), lambda bi, hp, qi: (bi * nq + qi, hp)),
        out_shape=jax.ShapeDtypeStruct((n, d), BF16),
        scratch_shapes=[pltpu.VMEM((2, t // tk, LANES, tk), BF16), pltpu.VMEM((2, t, LANES), BF16)],
        compiler_params=_params("parallel", "parallel", "arbitrary"),
        name="fox_attn_prompt",
    )(q, k, v, c)


def _fox_attn_sample_kernel(q_ref, kn_ref, vn_ref, kc_ref, vc_ref, c_ref, o_ref):
    hp = pl.program_id(1)
    t = q_ref.shape[0]
    past = kc_ref.shape[0]
    c16 = c_ref[0]
    q2 = q_ref[...].astype(F32)
    rows = lax.broadcasted_iota(jnp.int32, (t, t), 0)
    cols = lax.broadcasted_iota(jnp.int32, (t, t), 1)
    nt = (((1,), (1,)), ((), ()))
    accs = []
    for hh in range(2):
        ccol = _head_col(c16, 2 * hp + hh)
        qaug = _aug(q2, ccol[past:], hh, True).astype(BF16)
        kc = _aug(kc_ref[...], ccol[:past], hh, False).astype(BF16)
        kn = _aug(kn_ref[...], ccol[past:], hh, False).astype(BF16)
        s_c = lax.dot_general(qaug, kc, nt, preferred_element_type=F32)
        s_n = lax.dot_general(qaug, kn, nt, preferred_element_type=F32)
        s_n = jnp.where(cols <= rows, s_n, -jnp.inf)
        m = jnp.maximum(jnp.max(s_c, axis=1, keepdims=True), jnp.max(s_n, axis=1, keepdims=True))
        p_c = jnp.exp(s_c - m).astype(BF16)
        p_n = jnp.exp(s_n - m).astype(BF16)
        acc = (jnp.dot(p_c, _v_aug(vc_ref[...], hh).astype(BF16), preferred_element_type=F32)
               + jnp.dot(p_n, _v_aug(vn_ref[...], hh).astype(BF16), preferred_element_type=F32))
        accs.append(acc)
    o_ref[...] = _finish_heads(accs[0], accs[1]).astype(o_ref.dtype)


def _fox_attn_sample(q, k_new, v_new, cache_k, cache_v, c_all, b, t):
    n, d = q.shape
    past = cache_k.shape[1]
    nhp = d // LANES
    new_spec = pl.BlockSpec((t, LANES), lambda bi, hp: (bi, hp))
    cache_spec = pl.BlockSpec((None, past, LANES), lambda bi, hp: (bi, 0, hp))
    return pl.pallas_call(
        _fox_attn_sample_kernel,
        grid=(b, nhp),
        in_specs=[new_spec, new_spec, new_spec, cache_spec, cache_spec,
                  pl.BlockSpec((1, past + t, FOX_HEADS), lambda bi, hp: (bi, 0, 0))],
        out_specs=new_spec,
        out_shape=jax.ShapeDtypeStruct((n, d), BF16),
        compiler_params=_params("parallel", "parallel"),
        name="fox_attn_sample",
    )(q, k_new, v_new, cache_k, cache_v, c_all)


def _ffn_kernel(x_ref, o_ref, wo_ref, g_ref, w1_ref, w3_ref, w2_ref, y_ref, h_s):
    j = pl.program_id(1)

    @pl.when(j == 0)
    def _():
        x1 = x_ref[...] + jnp.dot(o_ref[...], wo_ref[...], preferred_element_type=F32)
        y_ref[...] = x1
        h_s[...] = _rms(x1, g_ref[...]).astype(BF16)

    h = h_s[...]
    a = jnp.dot(h, w1_ref[...], preferred_element_type=F32)
    b = jnp.dot(h, w3_ref[...], preferred_element_type=F32)
    t = (_silu(a) * b).astype(BF16)
    y_ref[...] += jnp.dot(t, w2_ref[...], preferred_element_type=F32)


def _attn_out_ffn(x, o, w_o, norm_g, w1, w3, w2, tm, tf):
    n, d = x.shape
    dff = w1.shape[1]
    row = lambda i, j: (i, 0)
    const = lambda i, j: (0, 0)
    return pl.pallas_call(
        _ffn_kernel,
        grid=(n // tm, dff // tf),
        in_specs=[pl.BlockSpec((tm, d), row), pl.BlockSpec((tm, d), row), pl.BlockSpec((d, d), const),
                  pl.BlockSpec((1, d), const), pl.BlockSpec((d, tf), lambda i, j: (0, j)),
                  pl.BlockSpec((d, tf), lambda i, j: (0, j)), pl.BlockSpec((tf, d), lambda i, j: (j, 0))],
        out_specs=pl.BlockSpec((tm, d), row),
        out_shape=jax.ShapeDtypeStruct((n, d), F32),
        scratch_shapes=[pltpu.VMEM((tm, d), BF16)],
        compiler_params=_params("parallel", "arbitrary"),
        name="attn_out_ffn",
    )(x, o, w_o.astype(BF16), norm_g[None, :], w1.astype(BF16), w3.astype(BF16), w2.astype(BF16))


def _gdn_proj_kernel(x_ref, g_ref, w_ref, cw_ref, hist_ref, alog_ref, dtb_ref,
                     q_ref, k_ref, v_ref, gate_ref, aux_ref, hout_ref, pad_s, *, steps_per_seq):
    i = pl.program_id(0)
    tm, d = x_ref.shape
    cdim = 3 * d
    first = (i % steps_per_seq) == 0

    @pl.when(first)
    def _():
        pad_s[0:8, :] = hist_ref[0]

    @pl.when(jnp.logical_not(first))
    def _():
        pad_s[0:8, :] = pad_s[tm:tm + 8, :]

    h = _rms(x_ref[...], g_ref[...])
    y = jnp.dot(h.astype(BF16), w_ref[...], preferred_element_type=F32)
    pad_s[8:8 + tm, :] = y[:, :cdim]
    hout_ref[0] = pad_s[tm:tm + 8, :]

    for blk in range(cdim // LANES):
        cs = slice(blk * LANES, (blk + 1) * LANES)
        mixed = jnp.zeros((tm, LANES), F32)
        for w in range(GDN_CONV):
            mixed = mixed + cw_ref[w:w + 1, cs] * pad_s[8 - (GDN_CONV - 1) + w:8 - (GDN_CONV - 1) + w + tm, cs]
        mixed = _silu(mixed)
        part, hd = divmod(blk, GDN_HEADS)
        hs = slice(hd * LANES, (hd + 1) * LANES)
        if part == 2:
            v_ref[:, hs] = mixed
        else:
            nrm = mixed * lax.rsqrt(jnp.sum(mixed * mixed, axis=1, keepdims=True) + EPS)
            if part == 0:
                q_ref[:, hs] = nrm * (GDN_DH ** -0.5)
            else:
                k_ref[:, hs] = nrm

    gate_ref[...] = _silu(y[:, cdim:cdim + d])
    z = y[:, cdim + d:cdim + d + LANES]
    lane = lax.broadcasted_iota(jnp.int32, z.shape, 1)
    beta = 1.0 / (1.0 + jnp.exp(-z))
    gdec = -jnp.exp(alog_ref[...]) * _softplus(z + dtb_ref[...])
    aux_ref[...] = jnp.where(lane < GDN_HEADS, beta, gdec)


def _gdn_proj(x, hist, b, t, tm, norm_g, w_qkv, conv_w, w_beta, w_a, a_log, dt_bias, w_gate):
    n, d = x.shape
    cdim = 3 * d
    w_all = jnp.concatenate(
        [w_qkv, w_gate, w_beta, w_a, jnp.zeros((d, LANES - 2 * GDN_HEADS), F32)], axis=1).astype(BF16)
    hist8 = jnp.concatenate([jnp.zeros((b, 8 - (GDN_CONV - 1), cdim), F32), hist], axis=1)
    lane_vec = lambda a: jnp.concatenate(
        [jnp.zeros((GDN_HEADS,), F32), a, jnp.zeros((LANES - 2 * GDN_HEADS,), F32)])[None, :]
    steps = t // tm
    row = lambda i: (i, 0)
    const = lambda i: (0, 0)
    outs = pl.pallas_call(
        functools.partial(_gdn_proj_kernel, steps_per_seq=steps),
        grid=(n // tm,),
        in_specs=[pl.BlockSpec((tm, d), row), pl.BlockSpec((1, d), const),
                  pl.BlockSpec((d, 4 * d + LANES), const), pl.BlockSpec((GDN_CONV, cdim), const),
                  pl.BlockSpec((1, 8, cdim), lambda i: (i // steps, 0, 0)),
                  pl.BlockSpec((1, LANES), const), pl.BlockSpec((1, LANES), const)],
        out_specs=[pl.BlockSpec((tm, d), row), pl.BlockSpec((tm, d), row), pl.BlockSpec((tm, d), row),
                   pl.BlockSpec((tm, d), row), pl.BlockSpec((tm, LANES), row),
                   pl.BlockSpec((1, 8, cdim), lambda i: (i // steps, 0, 0))],
        out_shape=[jax.ShapeDtypeStruct((n, d), F32)] * 4 + [jax.ShapeDtypeStruct((n, LANES), F32),
                                                           jax.ShapeDtypeStruct((b, 8, cdim), F32)],
        scratch_shapes=[pltpu.VMEM((tm + 8, cdim), F32)],
        compiler_params=_params("arbitrary"),
        name="gdn_proj",
    )(x, norm_g[None, :], w_all, conv_w, hist8, lane_vec(a_log), lane_vec(dt_bias))
    q, k, v, gate, aux, hout = outs
    return q, k, v, gate, aux, hout[:, 8 - (GDN_CONV - 1):, :]


def _gdn_chunk_kernel(q_ref, k_ref, v_ref, gate_ref, aux_ref, s0_ref, og_ref, o_ref, s_ref, *, chunk):
    ci = pl.program_id(1)
    per = LANES // chunk
    ngroups = GDN_HEADS // per
    nsq = chunk.bit_length() - 2

    @pl.when(ci == 0)
    def _():
        s_ref[...] = s0_ref[...]

    aux = aux_ref[...]
    trow = lax.broadcasted_iota(jnp.int32, aux.shape, 0)
    gc = aux
    sh = 1
    while sh < chunk:
        gc = gc + jnp.where(trow >= sh, pltpu.roll(gc, sh, axis=0), 0.0)
        sh *= 2
    glast = gc[chunk - 1:chunk, :]

    bi = lax.broadcasted_iota(jnp.int32, (LANES, LANES), 0)
    bj = lax.broadcasted_iota(jnp.int32, (LANES, LANES), 1)
    same = (bi // chunk) == (bj // chunk)
    causal = same & (bi >= bj)
    strict = same & (bi > bj)
    eye = (bi == bj).astype(F32)
    nt = (((1,), (1,)), ((), ()))

    for grp in range(ngroups):
        heads = [grp * per + a for a in range(per)]
        stack = lambda ref: jnp.concatenate([ref[:, hd * LANES:(hd + 1) * LANES] for hd in heads], axis=0)
        col = lambda arr, off: jnp.concatenate([arr[:, off + hd:off + hd + 1] for hd in heads], axis=0)
        q2, k2, v2 = stack(q_ref), stack(k_ref), stack(v_ref)
        beta = col(aux, 0)
        gcol = col(gc, GDN_HEADS)
        gl = jnp.concatenate([jnp.broadcast_to(glast[:, GDN_HEADS + hd:GDN_HEADS + hd + 1], (chunk, 1))
                              for hd in heads], axis=0)
        grow = jnp.broadcast_to(gcol, (LANES, LANES)).T
        decay = jnp.exp(jnp.where(causal, gcol - grow, -jnp.inf))
        k2b = k2.astype(BF16)
        kk = lax.dot_general(k2b, k2b, nt, preferred_element_type=F32)
        x = -jnp.where(strict, beta * kk * decay, 0.0)
        xb = x.astype(BF16)
        p = jnp.dot(xb, xb, preferred_element_type=F32)
        tser = eye + x
        for it in range(nsq):
            pb = p.astype(BF16)
            tser = tser + jnp.dot(tser.astype(BF16), pb, preferred_element_type=F32)
            if it + 1 < nsq:
                p = jnp.dot(pb, pb, preferred_element_type=F32)
        egc = jnp.exp(gcol)
        rhs = jnp.concatenate([v2 * beta, k2 * (beta * egc)], axis=1)
        hi_ord = (tser - eye - x).astype(BF16)
        rb = rhs.astype(BF16)
        sol = (rhs + jnp.dot(xb, rb, preferred_element_type=F32)
               + jnp.dot(hi_ord, rb, preferred_element_type=F32))
        u2, w2 = sol[:, :LANES], sol[:, LANES:]
        attn = lax.dot_general(q2.astype(BF16), k2b, nt, preferred_element_type=F32) * decay
        qd = (q2 * egc).astype(BF16)
        kd = k2 * jnp.exp(gl - gcol)
        kdt = kd.T
        w2b = w2.astype(BF16)
        ws, qs = [], []
        for a, hd in enumerate(heads):
            sb = s_ref[0, hd].astype(BF16)
            rs = slice(a * chunk, (a + 1) * chunk)
            ws.append(jnp.dot(w2b[rs], sb, preferred_element_type=F32))
            qs.append(jnp.dot(qd[rs], sb, preferred_element_type=F32))
        vnew = u2 - jnp.concatenate(ws, axis=0)
        vnb = vnew.astype(BF16)
        o2 = jnp.concatenate(qs, axis=0) + jnp.dot(attn.astype(BF16), vnb, preferred_element_type=F32)
        tcol = lax.broadcasted_iota(jnp.int32, (LANES, LANES), 1) // chunk
        for a, hd in enumerate(heads):
            rs = slice(a * chunk, (a + 1) * chunk)
            kdt_h = jnp.where(tcol == a, kdt, 0.0).astype(BF16)
            decay_s = jnp.exp(glast[:, GDN_HEADS + hd:GDN_HEADS + hd + 1])
            s_ref[0, hd] = s_ref[0, hd] * decay_s + jnp.dot(kdt_h, vnb, preferred_element_type=F32)
            oh = _rms(o2[rs], og_ref[...]) * gate_ref[:, hd * LANES:(hd + 1) * LANES]
            o_ref[:, hd * LANES:(hd + 1) * LANES] = oh.astype(o_ref.dtype)


def _gdn_chunks(q, k, v, gate, aux, s0, out_gain, b, t, chunk):
    n, d = q.shape
    nc = t // chunk
    row = lambda bi, ci: (bi * nc + ci, 0)
    st = lambda bi, ci: (bi, 0, 0, 0)
    sshape = (1, GDN_HEADS, GDN_DH, GDN_DH)
    return pl.pallas_call(
        functools.partial(_gdn_chunk_kernel, chunk=chunk),
        grid=(b, nc),
        in_specs=[pl.BlockSpec((chunk, d), row)] * 4 + [pl.BlockSpec((chunk, LANES), row),
                                                       pl.BlockSpec(sshape, st),
                                                       pl.BlockSpec((1, GDN_DH), lambda bi, ci: (0, 0))],
        out_specs=[pl.BlockSpec((chunk, d), row), pl.BlockSpec(sshape, st)],
        out_shape=[jax.ShapeDtypeStruct((n, d), BF16), jax.ShapeDtypeStruct(s0.shape, F32)],
        compiler_params=_params("parallel", "arbitrary"),
        name="gdn_chunks",
    )(q, k, v, gate, aux, s0, out_gain[None, :])


def _router_kernel(x_ref, o_ref, wo_ref, g_ref, r_ref, x1_ref, h_ref, comb_ref):
    x1 = x_ref[...] + jnp.dot(o_ref[...], wo_ref[...], preferred_element_type=F32)
    x1_ref[...] = x1
    h = _rms(x1, g_ref[...])
    h_ref[...] = h.astype(h_ref.dtype)
    logits = jnp.dot(h, r_ref[...], preferred_element_type=F32, precision=HIGHEST)
    lane = lax.broadcasted_iota(jnp.int32, logits.shape, 1)
    logits = jnp.where(lane < N_EXPERTS, logits, -jnp.inf)
    m1 = jnp.max(logits, axis=1, keepdims=True)
    i1 = jnp.min(jnp.where(logits == m1, lane, LANES), axis=1, keepdims=True)
    rest = jnp.where(lane == i1, -jnp.inf, logits)
    m2 = jnp.max(rest, axis=1, keepdims=True)
    i2 = jnp.min(jnp.where(rest == m2, lane, LANES), axis=1, keepdims=True)
    e2 = jnp.exp(m2 - m1)
    g1 = 1.0 / (1.0 + e2)
    g2 = e2 / (1.0 + e2)
    comb_ref[...] = jnp.where(lane == i1, g1, jnp.where(lane == i2, g2, 0.0))


def _moe_router(x, o, w_o, norm_g, router, tm):
    n, d = x.shape
    r_pad = jnp.concatenate([router, jnp.zeros((d, LANES - N_EXPERTS), F32)], axis=1)
    row = lambda i: (i, 0)
    const = lambda i: (0, 0)
    return pl.pallas_call(
        _router_kernel,
        grid=(n // tm,),
        in_specs=[pl.BlockSpec((tm, d), row), pl.BlockSpec((tm, d), row), pl.BlockSpec((d, d), const),
                  pl.BlockSpec((1, d), const), pl.BlockSpec((d, LANES), const)],
        out_specs=[pl.BlockSpec((tm, d), row), pl.BlockSpec((tm, d), row), pl.BlockSpec((tm, LANES), row)],
        out_shape=[jax.ShapeDtypeStruct((n, d), F32), jax.ShapeDtypeStruct((n, d), BF16),
                   jax.ShapeDtypeStruct((n, LANES), F32)],
        compiler_params=_params("parallel"),
        name="moe_router",
    )(x, o, w_o.astype(BF16), norm_g[None, :], r_pad)


def _moe_dense_kernel(x1_ref, h_ref, comb_ref, w1_ref, w3_ref, w2_ref, y_ref):
    e = pl.program_id(1)
    j = pl.program_id(2)

    @pl.when((e == 0) & (j == 0))
    def _():
        y_ref[...] = x1_ref[...]

    comb = comb_ref[...]
    lane = lax.broadcasted_iota(jnp.int32, comb.shape, 1)
    ce = jnp.sum(jnp.where(lane == e, comb, 0.0), axis=1, keepdims=True)
    h = h_ref[...]
    a = jnp.dot(h, w1_ref[0], preferred_element_type=F32)
    b = jnp.dot(h, w3_ref[0], preferred_element_type=F32)
    t = (_silu(a) * b).astype(BF16)
    y_ref[...] += ce * jnp.dot(t, w2_ref[0], preferred_element_type=F32)


def _moe_dense(x1, h, comb, w1, w3, w2, tm, tf):
    n, d = x1.shape
    ne, _, dff = w1.shape
    row = lambda i, e, j: (i, 0)
    return pl.pallas_call(
        _moe_dense_kernel,
        grid=(n // tm, ne, dff // tf),
        in_specs=[pl.BlockSpec((tm, d), row), pl.BlockSpec((tm, d), row), pl.BlockSpec((tm, LANES), row),
                  pl.BlockSpec((1, d, tf), lambda i, e, j: (e, 0, j)),
                  pl.BlockSpec((1, d, tf), lambda i, e, j: (e, 0, j)),
                  pl.BlockSpec((1, tf, d), lambda i, e, j: (e, j, 0))],
        out_specs=pl.BlockSpec((tm, d), row),
        out_shape=jax.ShapeDtypeStruct((n, d), F32),
        compiler_params=_params("parallel", "arbitrary", "arbitrary"),
        name="moe_dense",
    )(x1, h, comb, w1.astype(BF16), w3.astype(BF16), w2.astype(BF16))


def _fox_layer(x, b, t, cache, w, ffn_w):
    n, d = x.shape
    tm = min(512, n)
    q, k, v, logf = _fox_proj(x, *w[:6], tm=tm)
    if cache is None:
        c = _cumsum_time(logf.reshape(b, t, FOX_HEADS))
        o = _fox_attn_prompt(q, k, v, c, b, t, tq=min(512, t), tk=min(256, t))
    else:
        ck, cv, cl = cache
        past = ck.shape[1]
        c = _cumsum_time(jnp.concatenate([cl, logf.reshape(b, t, FOX_HEADS)], axis=1))
        o = _fox_attn_sample(q, k, v, ck.reshape(b, past, d), cv.reshape(b, past, d), c, b, t)
    y = _attn_out_ffn(x, o, w[6], *ffn_w, tm=tm, tf=512)
    return y, k, v, logf


def _gdn_layer(x, b, t, chunk, hist, s0, w, moe_w):
    n, d = x.shape
    norm_g, w_qkv, conv_w, w_beta, w_a, a_log, dt_bias, w_gate, out_gain, w_o = w
    tm = min(512, t)
    q, k, v, gate, aux, new_hist = _gdn_proj(x, hist, b, t, tm, norm_g, w_qkv, conv_w, w_beta, w_a,
                                             a_log, dt_bias, w_gate)
    o, s_new = _gdn_chunks(q, k, v, gate, aux, s0, out_gain, b, t, chunk)
    moe_norm, router, w1, w3, w2 = moe_w
    x1, h, comb = _moe_router(x, o, w_o, moe_norm, router, tm=min(512, n))
    y = _moe_dense(x1, h, comb, w1, w3, w2, tm=min(512, n), tf=512)
    return y, s_new, new_hist


def kernel(x_prompt, x_sample, cache_fox_k, cache_fox_v, cache_fox_logf, state_gdn_s, state_gdn_conv, fox_norm, fox_w_qkv, fox_q_gain, fox_k_gain, fox_w_f, fox_b_f, fox_w_o, ffn_norm, ffn_w1, ffn_w3, ffn_w2, gdn_norm, gdn_w_qkv, gdn_conv_w, gdn_w_beta, gdn_w_a, gdn_a_log, gdn_dt_bias, gdn_w_gate, gdn_out_gain, gdn_w_o, moe_norm, moe_router, moe_w1, moe_w3, moe_w2):
    bp, tp, d = x_prompt.shape
    bs, ts, _ = x_sample.shape
    depth = fox_norm.shape[0] + gdn_norm.shape[0]
    cdim = gdn_w_qkv.shape[2]
    yp = x_prompt.reshape(bp * tp, d)
    ys = x_sample.reshape(bs * ts, d)
    kp_l, vp_l, lp_l, ks_l, vs_l, ls_l = [], [], [], [], [], []
    sp_l, cp_l, ss_l, cs_l = [], [], [], []
    for i in range(depth):
        j = i // 2
        if i % 2 == 0:
            fw = (fox_norm[j], fox_w_qkv[j], fox_q_gain[j], fox_k_gain[j], fox_w_f[j], fox_b_f[j], fox_w_o[j])
            ffw = (ffn_norm[j], ffn_w1[j], ffn_w3[j], ffn_w2[j])
            yp, kp, vp, lp = _fox_layer(yp, bp, tp, None, fw, ffw)
            ys, ks, vs, ls = _fox_layer(ys, bs, ts, (cache_fox_k[j], cache_fox_v[j], cache_fox_logf[j]), fw, ffw)
            kp_l.append(kp.reshape(bp, tp, FOX_HEADS, FOX_DH)); vp_l.append(vp.reshape(bp, tp, FOX_HEADS, FOX_DH))
            lp_l.append(lp.reshape(bp, tp, FOX_HEADS))
            ks_l.append(ks.reshape(bs, ts, FOX_HEADS, FOX_DH)); vs_l.append(vs.reshape(bs, ts, FOX_HEADS, FOX_DH))
            ls_l.append(ls.reshape(bs, ts, FOX_HEADS))
        else:
            gw = (gdn_norm[j], gdn_w_qkv[j], gdn_conv_w[j], gdn_w_beta[j], gdn_w_a[j], gdn_a_log[j],
                  gdn_dt_bias[j], gdn_w_gate[j], gdn_out_gain[j], gdn_w_o[j])
            mw = (moe_norm[j], moe_router[j], moe_w1[j], moe_w3[j], moe_w2[j])
            hist0 = jnp.zeros((bp, GDN_CONV - 1, cdim), F32)
            s_zero = jnp.zeros((bp, GDN_HEADS, GDN_DH, GDN_DH), F32)
            yp, sp, cp = _gdn_layer(yp, bp, tp, 64, hist0, s_zero, gw, mw)
            ys, ss, cs = _gdn_layer(ys, bs, ts, ts, state_gdn_conv[j], state_gdn_s[j], gw, mw)
            sp_l.append(sp); cp_l.append(cp); ss_l.append(ss); cs_l.append(cs)
    return (yp.reshape(bp, tp, d), ys.reshape(bs, ts, d),
            jnp.stack(kp_l), jnp.stack(vp_l), jnp.stack(lp_l),
            jnp.stack(ks_l), jnp.stack(vs_l), jnp.stack(ls_l),
            jnp.stack(sp_l), jnp.stack(cp_l), jnp.stack(ss_l), jnp.stack(cs_l))
```

```python
import functools

import jax
import jax.numpy as jnp
from jax import lax
from jax.experimental import pallas as pl
from jax.experimental.pallas import tpu as pltpu

F32 = jnp.float32
BF16 = jnp.bfloat16
EPS = 1e-6
LANES = 128
VMEM_LIMIT = 56 * 1024 * 1024

FOX_HEADS = 16
FOX_DH = 64
GDN_HEADS = 8
GDN_DH = 128
GDN_CONV = 4
N_EXPERTS = 8
GDN_CHAINS_PER_STEP = 16
HIGHEST = lax.Precision.HIGHEST


def _params(*sem):
    return pltpu.CompilerParams(dimension_semantics=sem, vmem_limit_bytes=VMEM_LIMIT)


def _rms(x, gain):
    return x * lax.rsqrt(jnp.mean(x * x, axis=-1, keepdims=True) + EPS) * gain


def _silu(x):
    return x * (1.0 / (1.0 + jnp.exp(-x)))


def _softplus(x):
    return jnp.maximum(x, 0.0) + jnp.log(1.0 + jnp.exp(-jnp.abs(x)))


def _split3(c):
    hi = c.astype(BF16).astype(F32)
    r = c - hi
    mid = r.astype(BF16).astype(F32)
    lo = (r - mid).astype(BF16).astype(F32)
    return hi, mid, lo


def _fox_proj_kernel(x_ref, g_ref, w_ref, qg_ref, kg_ref, bf_ref, gm_ref, gmt_ref,
                     q_ref, k_ref, v_ref, lf_ref):
    d = x_ref.shape[1]
    h = _rms(x_ref[...], g_ref[...])
    y = jnp.dot(h.astype(BF16), w_ref[...], preferred_element_type=F32)

    def headnorm(z, gain):
        ss = jnp.dot((z * z).astype(BF16), gm_ref[...], preferred_element_type=F32)
        inv = lax.rsqrt(ss * (1.0 / FOX_DH) + EPS)
        hi = inv.astype(BF16)
        lo = (inv - hi.astype(F32)).astype(BF16)
        full = (jnp.dot(hi, gmt_ref[...], preferred_element_type=F32)
                + jnp.dot(lo, gmt_ref[...], preferred_element_type=F32))
        return z * full * gain

    q_ref[...] = (headnorm(y[:, :d], qg_ref[...]) * (FOX_DH ** -0.5)).astype(q_ref.dtype)
    k_ref[...] = headnorm(y[:, d:2 * d], kg_ref[...])
    v_ref[...] = y[:, 2 * d:3 * d]
    z = y[:, 3 * d:3 * d + LANES] + bf_ref[...]
    lf = jnp.minimum(z, 0.0) - jnp.log(1.0 + jnp.exp(-jnp.abs(z)))
    lf_ref[...] = lf[:, :FOX_HEADS]


def _fox_proj(x, norm_g, w_qkv, q_gain, k_gain, w_f, b_f, tm):
    n, d = x.shape
    w_all = jnp.concatenate(
        [w_qkv, w_f, jnp.zeros((d, LANES - FOX_HEADS), F32)], axis=1).astype(BF16)
    bf = jnp.concatenate([b_f, jnp.zeros((LANES - FOX_HEADS,), F32)])[None, :]
    head_of_col = jnp.arange(d) // FOX_DH
    gm = (head_of_col[:, None] == jnp.arange(LANES)[None, :]).astype(BF16)
    gmt = gm.T
    qg = jnp.tile(q_gain, FOX_HEADS)[None, :]
    kg = jnp.tile(k_gain, FOX_HEADS)[None, :]
    const = lambda i: (0, 0)
    row = lambda i: (i, 0)
    return pl.pallas_call(
        _fox_proj_kernel,
        grid=(n // tm,),
        in_specs=[pl.BlockSpec((tm, d), row), pl.BlockSpec((1, d), const),
                  pl.BlockSpec((d, 3 * d + LANES), const), pl.BlockSpec((1, d), const),
                  pl.BlockSpec((1, d), const), pl.BlockSpec((1, LANES), const),
                  pl.BlockSpec((d, LANES), const), pl.BlockSpec((LANES, d), const)],
        out_specs=[pl.BlockSpec((tm, d), row), pl.BlockSpec((tm, d), row), pl.BlockSpec((tm, d), row),
                   pl.BlockSpec((tm, FOX_HEADS), row)],
        out_shape=[jax.ShapeDtypeStruct((n, d), BF16), jax.ShapeDtypeStruct((n, d), F32),
                   jax.ShapeDtypeStruct((n, d), F32), jax.ShapeDtypeStruct((n, FOX_HEADS), F32)],
        compiler_params=_params("parallel"),
        name="fox_proj",
    )(x, norm_g[None, :], w_all, qg, kg, bf, gm, gmt)


def _cumsum_kernel(x_ref, o_ref):
    x = x_ref[0]
    t = x.shape[0]
    row = lax.broadcasted_iota(jnp.int32, x.shape, 0)
    s = 1
    while s < t:
        x = x + jnp.where(row >= s, pltpu.roll(x, s, axis=0), 0.0)
        s *= 2
    o_ref[0] = x


def _cumsum_time(x):
    b, t, hn = x.shape
    spec = pl.BlockSpec((1, t, hn), lambda i: (i, 0, 0))
    return pl.pallas_call(
        _cumsum_kernel, grid=(b,), in_specs=[spec], out_specs=spec,
        out_shape=jax.ShapeDtypeStruct(x.shape, F32),
        compiler_params=_params("parallel"), name="logf_cumsum",
    )(x)


def _head_col(c16, head):
    lane = lax.broadcasted_iota(jnp.int32, c16.shape, 1)
    return jnp.sum(jnp.where(lane == head, c16, 0.0), axis=1, keepdims=True)


def _aug(data, ccol, hh, is_q):
    lane = lax.broadcasted_iota(jnp.int32, data.shape, 1)
    base = FOX_DH if hh == 0 else 0
    in_data = (lane < FOX_DH) if hh == 0 else (lane >= FOX_DH)
    hi, mid, lo = _split3(ccol)
    lane1 = lax.broadcasted_iota(jnp.int32, (1, data.shape[1]), 1)
    if is_q:
        pieces = {0: hi, 1: mid, 2: lo}
        ones = (lane1 >= base + 3) & (lane1 < base + 6)
    else:
        pieces = {3: -hi, 4: -mid, 5: -lo}
        ones = (lane1 >= base) & (lane1 < base + 3)
    aug = jnp.where(ones, 1.0, 0.0)
    for i, p in pieces.items():
        aug = jnp.where(lane == base + i, p, aug)
    return jnp.where(in_data, data, aug)


def _v_aug(v2, hh):
    lane = lax.broadcasted_iota(jnp.int32, v2.shape, 1)
    in_data = (lane < FOX_DH) if hh == 0 else (lane >= FOX_DH)
    return jnp.where(in_data, v2, 1.0)


def _finish_heads(acc0, acc1):
    lane = lax.broadcasted_iota(jnp.int32, acc0.shape, 1)
    o0 = acc0 / pltpu.roll(acc0, FOX_DH, axis=1)
    o1 = acc1 / pltpu.roll(acc1, FOX_DH, axis=1)
    return jnp.where(lane < FOX_DH, o0, o1)


def _fox_attn_kernel(q_ref, k_ref, v_ref, c_ref, o_ref, kt_s, v_s, *, tq, tk):
    hp = pl.program_id(1)
    t = k_ref.shape[0]
    nkb = t // tk
    nd = tq // tk
    c16 = c_ref[0]
    k2 = k_ref[...]
    v2 = v_ref[...]
    ccols = [_head_col(c16, 2 * hp + hh) for hh in range(2)]
    for hh in range(2):
        kaug = _aug(k2, ccols[hh], hh, False)
        for jb in range(nkb):
            kt_s[hh, jb] = kaug[jb * tk:(jb + 1) * tk, :].T.astype(BF16)
        v_s[hh] = _v_aug(v2, hh).astype(BF16)

    rows = lax.broadcasted_iota(jnp.int32, (tq, tk), 0)
    cols = lax.broadcasted_iota(jnp.int32, (tq, tk), 1)
    chains = []
    for i in range(t // tq):
        q2 = q_ref[i * tq:(i + 1) * tq, :].astype(F32)
        for hh in range(2):
            qb = _aug(q2, ccols[hh][i * tq:(i + 1) * tq], hh, True).astype(BF16)
            order = list(range(i * nd, (i + 1) * nd)) + list(range(i * nd))
            chains.append(dict(i=i, hh=hh, qb=qb, order=order, m=None, acc=None))
    for rnd in range(max(len(c["order"]) for c in chains)):
        live = [c for c in chains if rnd < len(c["order"])]
        for c in live:
            j = c["order"][rnd]
            s = jnp.dot(c["qb"], kt_s[c["hh"], j], preferred_element_type=F32)
            dblk = j - c["i"] * nd
            if dblk >= 0:
                s = jnp.where(cols + dblk * tk <= rows, s, -jnp.inf)
            c["s"] = s
        for c in live:
            bm = jnp.max(c["s"], axis=1, keepdims=True)
            m_new = bm if c["m"] is None else jnp.maximum(c["m"], bm)
            c["p"] = jnp.exp(c["s"] - m_new).astype(BF16)
            c["alpha"] = None if c["m"] is None else jnp.exp(c["m"] - m_new)
            c["m"] = m_new
        for c in live:
            j = c["order"][rnd]
            pv = jnp.dot(c["p"], v_s[c["hh"], j * tk:(j + 1) * tk, :], preferred_element_type=F32)
            c["acc"] = pv if c["acc"] is None else c["alpha"] * c["acc"] + pv
    for i in range(t // tq):
        o_ref[i * tq:(i + 1) * tq, :] = _finish_heads(chains[2 * i]["acc"],
                                                       chains[2 * i + 1]["acc"]).astype(o_ref.dtype)


def _fox_attn_prompt(q, k, v, c, b, t, tq, tk):
    n, d = q.shape
    nhp = d // LANES
    blk = pl.BlockSpec((t, LANES), lambda bi, hp: (bi, hp))
    return pl.pallas_call(
        functools.partial(_fox_attn_kernel, tq=tq, tk=tk),
        grid=(b, nhp),
        in_specs=[blk, blk, blk, pl.BlockSpec((1, t, FOX_HEADS), lambda bi, hp: (bi, 0, 0))],
        out_specs=blk,
        out_shape=jax.ShapeDtypeStruct((n, d), BF16),
        scratch_shapes=[pltpu.VMEM((2, t // tk, LANES, tk), BF16), pltpu.VMEM((2, t, LANES), BF16)],
        compiler_params=_params("parallel", "parallel"),
        name="fox_attn_prompt",
    )(q, k, v, c)


def _fox_attn_sample_kernel(q_ref, kn_ref, vn_ref, kc_ref, vc_ref, c_ref, o_ref):
    hp = pl.program_id(1)
    t = q_ref.shape[0]
    past = kc_ref.shape[0]
    c16 = c_ref[0]
    q2 = q_ref[...].astype(F32)
    rows = lax.broadcasted_iota(jnp.int32, (t, t), 0)
    cols = lax.broadcasted_iota(jnp.int32, (t, t), 1)
    nt = (((1,), (1,)), ((), ()))
    accs = []
    for hh in range(2):
        ccol = _head_col(c16, 2 * hp + hh)
        qaug = _aug(q2, ccol[past:], hh, True).astype(BF16)
        kc = _aug(kc_ref[...], ccol[:past], hh, False).astype(BF16)
        kn = _aug(kn_ref[...], ccol[past:], hh, False).astype(BF16)
        s_c = lax.dot_general(qaug, kc, nt, preferred_element_type=F32)
        s_n = lax.dot_general(qaug, kn, nt, preferred_element_type=F32)
        s_n = jnp.where(cols <= rows, s_n, -jnp.inf)
        m = jnp.maximum(jnp.max(s_c, axis=1, keepdims=True), jnp.max(s_n, axis=1, keepdims=True))
        p_c = jnp.exp(s_c - m).astype(BF16)
        p_n = jnp.exp(s_n - m).astype(BF16)
        acc = (jnp.dot(p_c, _v_aug(vc_ref[...], hh).astype(BF16), preferred_element_type=F32)
               + jnp.dot(p_n, _v_aug(vn_ref[...], hh).astype(BF16), preferred_element_type=F32))
        accs.append(acc)
    o_ref[...] = _finish_heads(accs[0], accs[1]).astype(o_ref.dtype)


def _fox_attn_sample(q, k_new, v_new, cache_k, cache_v, c_all, b, t):
    n, d = q.shape
    past = cache_k.shape[1]
    nhp = d // LANES
    new_spec = pl.BlockSpec((t, LANES), lambda bi, hp: (bi, hp))
    cache_spec = pl.BlockSpec((None, past, LANES), lambda bi, hp: (bi, 0, hp))
    return pl.pallas_call(
        _fox_attn_sample_kernel,
        grid=(b, nhp),
        in_specs=[new_spec, new_spec, new_spec, cache_spec, cache_spec,
                  pl.BlockSpec((1, past + t, FOX_HEADS), lambda bi, hp: (bi, 0, 0))],
        out_specs=new_spec,
        out_shape=jax.ShapeDtypeStruct((n, d), BF16),
        compiler_params=_params("parallel", "parallel"),
        name="fox_attn_sample",
    )(q, k_new, v_new, cache_k, cache_v, c_all)


def _ffn_kernel(x_ref, o_ref, wo_ref, g_ref, w1_ref, w3_ref, w2_ref, y_ref, h_s):
    j = pl.program_id(1)

    @pl.when(j == 0)
    def _():
        x1 = x_ref[...] + jnp.dot(o_ref[...], wo_ref[...], preferred_element_type=F32)
        y_ref[...] = x1
        h_s[...] = _rms(x1, g_ref[...]).astype(BF16)

    h = h_s[...]
    a = jnp.dot(h, w1_ref[...], preferred_element_type=F32)
    b = jnp.dot(h, w3_ref[...], preferred_element_type=F32)
    t = (_silu(a) * b).astype(BF16)
    y_ref[...] += jnp.dot(t, w2_ref[...], preferred_element_type=F32)


def _attn_out_ffn(x, o, w_o, norm_g, w1, w3, w2, tm, tf):
    n, d = x.shape
    dff = w1.shape[1]
    row = lambda i, j: (i, 0)
    const = lambda i, j: (0, 0)
    return pl.pallas_call(
        _ffn_kernel,
        grid=(n // tm, dff // tf),
        in_specs=[pl.BlockSpec((tm, d), row), pl.BlockSpec((tm, d), row), pl.BlockSpec((d, d), const),
                  pl.BlockSpec((1, d), const), pl.BlockSpec((d, tf), lambda i, j: (0, j)),
                  pl.BlockSpec((d, tf), lambda i, j: (0, j)), pl.BlockSpec((tf, d), lambda i, j: (j, 0))],
        out_specs=pl.BlockSpec((tm, d), row),
        out_shape=jax.ShapeDtypeStruct((n, d), F32),
        scratch_shapes=[pltpu.VMEM((tm, d), BF16)],
        compiler_params=_params("parallel", "arbitrary"),
        name="attn_out_ffn",
    )(x, o, w_o.astype(BF16), norm_g[None, :], w1.astype(BF16), w3.astype(BF16), w2.astype(BF16))


def _gdn_proj_kernel(x_ref, g_ref, w_ref, cw_ref, hist_ref, alog_ref, dtb_ref,
                     q_ref, k_ref, v_ref, gate_ref, aux_ref, hout_ref, pad_s, *, steps_per_seq):
    i = pl.program_id(0)
    tm, d = x_ref.shape
    cdim = 3 * d
    first = (i % steps_per_seq) == 0

    @pl.when(first)
    def _():
        pad_s[0:8, :] = hist_ref[0]

    @pl.when(jnp.logical_not(first))
    def _():
        pad_s[0:8, :] = pad_s[tm:tm + 8, :]

    h = _rms(x_ref[...], g_ref[...])
    y = jnp.dot(h.astype(BF16), w_ref[...], preferred_element_type=F32)
    pad_s[8:8 + tm, :] = y[:, :cdim]
    hout_ref[0] = pad_s[tm:tm + 8, :]

    for blk in range(cdim // LANES):
        cs = slice(blk * LANES, (blk + 1) * LANES)
        mixed = jnp.zeros((tm, LANES), F32)
        for w in range(GDN_CONV):
            mixed = mixed + cw_ref[w:w + 1, cs] * pad_s[8 - (GDN_CONV - 1) + w:8 - (GDN_CONV - 1) + w + tm, cs]
        mixed = _silu(mixed)
        part, hd = divmod(blk, GDN_HEADS)
        hs = slice(hd * LANES, (hd + 1) * LANES)
        if part == 2:
            v_ref[:, hs] = mixed
        else:
            nrm = mixed * lax.rsqrt(jnp.sum(mixed * mixed, axis=1, keepdims=True) + EPS)
            if part == 0:
                q_ref[:, hs] = nrm * (GDN_DH ** -0.5)
            else:
                k_ref[:, hs] = nrm

    gate_ref[...] = _silu(y[:, cdim:cdim + d])
    z = y[:, cdim + d:cdim + d + LANES]
    lane = lax.broadcasted_iota(jnp.int32, z.shape, 1)
    beta = 1.0 / (1.0 + jnp.exp(-z))
    gdec = -jnp.exp(alog_ref[...]) * _softplus(z + dtb_ref[...])
    aux_ref[...] = jnp.where(lane < GDN_HEADS, beta, gdec)


def _gdn_proj(x, hist, b, t, tm, norm_g, w_qkv, conv_w, w_beta, w_a, a_log, dt_bias, w_gate):
    n, d = x.shape
    cdim = 3 * d
    w_all = jnp.concatenate(
        [w_qkv, w_gate, w_beta, w_a, jnp.zeros((d, LANES - 2 * GDN_HEADS), F32)], axis=1).astype(BF16)
    hist8 = jnp.concatenate([jnp.zeros((b, 8 - (GDN_CONV - 1), cdim), F32), hist], axis=1)
    lane_vec = lambda a: jnp.concatenate(
        [jnp.zeros((GDN_HEADS,), F32), a, jnp.zeros((LANES - 2 * GDN_HEADS,), F32)])[None, :]
    steps = t // tm
    row = lambda i: (i, 0)
    const = lambda i: (0, 0)
    outs = pl.pallas_call(
        functools.partial(_gdn_proj_kernel, steps_per_seq=steps),
        grid=(n // tm,),
        in_specs=[pl.BlockSpec((tm, d), row), pl.BlockSpec((1, d), const),
                  pl.BlockSpec((d, 4 * d + LANES), const), pl.BlockSpec((GDN_CONV, cdim), const),
                  pl.BlockSpec((1, 8, cdim), lambda i: (i // steps, 0, 0)),
                  pl.BlockSpec((1, LANES), const), pl.BlockSpec((1, LANES), const)],
        out_specs=[pl.BlockSpec((tm, d), row), pl.BlockSpec((tm, d), row), pl.BlockSpec((tm, d), row),
                   pl.BlockSpec((tm, d), row), pl.BlockSpec((tm, LANES), row),
                   pl.BlockSpec((1, 8, cdim), lambda i: (i // steps, 0, 0))],
        out_shape=[jax.ShapeDtypeStruct((n, d), F32)] * 4 + [jax.ShapeDtypeStruct((n, LANES), F32),
                                                           jax.ShapeDtypeStruct((b, 8, cdim), F32)],
        scratch_shapes=[pltpu.VMEM((tm + 8, cdim), F32)],
        compiler_params=_params("arbitrary"),
        name="gdn_proj",
    )(x, norm_g[None, :], w_all, conv_w, hist8, lane_vec(a_log), lane_vec(dt_bias))
    q, k, v, gate, aux, hout = outs
    return q, k, v, gate, aux, hout[:, 8 - (GDN_CONV - 1):, :]


def _gdn_chunk_kernel(q_ref, k_ref, v_ref, gate_ref, aux_ref, s0_ref, og_ref, o_ref, s_ref, *, chunk):
    ci = pl.program_id(1)
    nb = q_ref.shape[0]
    per = LANES // chunk
    ngroups = GDN_HEADS // per
    nsq = chunk.bit_length() - 2

    @pl.when(ci == 0)
    def _():
        s_ref[...] = s0_ref[...]

    bi = lax.broadcasted_iota(jnp.int32, (LANES, LANES), 0)
    bj = lax.broadcasted_iota(jnp.int32, (LANES, LANES), 1)
    same = (bi // chunk) == (bj // chunk)
    causal = same & (bi >= bj)
    strict = same & (bi > bj)
    eye = (bi == bj).astype(F32)
    tcol = bj // chunk
    nt = (((1,), (1,)), ((), ()))

    chains = []
    for bb in range(nb):
        aux = aux_ref[bb]
        trow = lax.broadcasted_iota(jnp.int32, aux.shape, 0)
        gc = aux
        sh = 1
        while sh < chunk:
            gc = gc + jnp.where(trow >= sh, pltpu.roll(gc, sh, axis=0), 0.0)
            sh *= 2
        glast = gc[chunk - 1:chunk, :]
        for grp in range(ngroups):
            heads = [grp * per + a for a in range(per)]
            stack = lambda ref: jnp.concatenate(
                [ref[bb, :, hd * LANES:(hd + 1) * LANES] for hd in heads], axis=0)
            col = lambda arr, off: jnp.concatenate([arr[:, off + hd:off + hd + 1] for hd in heads], axis=0)
            c = dict(bb=bb, heads=heads, glast=glast)
            c["q2"], c["k2"], c["v2"] = stack(q_ref), stack(k_ref), stack(v_ref)
            c["beta"] = col(aux, 0)
            gcol = col(gc, GDN_HEADS)
            gl = jnp.concatenate([jnp.broadcast_to(glast[:, GDN_HEADS + hd:GDN_HEADS + hd + 1], (chunk, 1))
                                  for hd in heads], axis=0)
            grow = jnp.broadcast_to(gcol, (LANES, LANES)).T
            c["decay"] = jnp.exp(jnp.where(causal, gcol - grow, -jnp.inf))
            c["egc"] = jnp.exp(gcol)
            c["kdec"] = jnp.exp(gl - gcol)
            chains.append(c)

    for c in chains:
        k2b = c["k2"].astype(BF16)
        c["both"] = lax.dot_general(jnp.concatenate([k2b, c["q2"].astype(BF16)], axis=0), k2b, nt,
                                    preferred_element_type=F32)
    for c in chains:
        c["x"] = -jnp.where(strict, c["beta"] * c["both"][:LANES] * c["decay"], 0.0)
        c["attn"] = (c["both"][LANES:] * c["decay"]).astype(BF16)
        xb = c["x"].astype(BF16)
        c["p"] = jnp.dot(xb, xb, preferred_element_type=F32)
        c["tser"] = eye + c["x"]
    for it in range(nsq):
        for c in chains:
            c["pb"] = c["p"].astype(BF16)
            c["tser"] = c["tser"] + jnp.dot(c["tser"].astype(BF16), c["pb"], preferred_element_type=F32)
        if it + 1 < nsq:
            for c in chains:
                c["p"] = jnp.dot(c["pb"], c["pb"], preferred_element_type=F32)
    for c in chains:
        rhs = jnp.concatenate([c["v2"] * c["beta"], c["k2"] * (c["beta"] * c["egc"])], axis=1)
        c["sol"] = rhs + jnp.dot((c["tser"] - eye).astype(BF16), rhs.astype(BF16), preferred_element_type=F32)
    for c in chains:
        w2b = c["sol"][:, LANES:].astype(BF16)
        qd = (c["q2"] * c["egc"]).astype(BF16)
        c["wq"] = []
        for a, hd in enumerate(c["heads"]):
            rs = slice(a * chunk, (a + 1) * chunk)
            c["wq"].append(jnp.dot(jnp.concatenate([w2b[rs], qd[rs]], axis=0), s_ref[c["bb"], hd].astype(BF16),
                                   preferred_element_type=F32))
    for c in chains:
        ws = jnp.concatenate([wq[:chunk] for wq in c["wq"]], axis=0)
        c["qs"] = jnp.concatenate([wq[chunk:] for wq in c["wq"]], axis=0)
        c["vnb"] = (c["sol"][:, :LANES] - ws).astype(BF16)
        c["kdt"] = (c["k2"] * c["kdec"]).T
    for c in chains:
        c["o2"] = c["qs"] + jnp.dot(c["attn"], c["vnb"], preferred_element_type=F32)
        for a, hd in enumerate(c["heads"]):
            kdt_h = jnp.where(tcol == a, c["kdt"], 0.0).astype(BF16)
            decay_s = jnp.exp(c["glast"][:, GDN_HEADS + hd:GDN_HEADS + hd + 1])
            s_ref[c["bb"], hd] = (s_ref[c["bb"], hd] * decay_s
                                  + jnp.dot(kdt_h, c["vnb"], preferred_element_type=F32))
    for c in chains:
        for a, hd in enumerate(c["heads"]):
            rs = slice(a * chunk, (a + 1) * chunk)
            oh = _rms(c["o2"][rs], og_ref[...]) * gate_ref[c["bb"], :, hd * LANES:(hd + 1) * LANES]
            o_ref[c["bb"], :, hd * LANES:(hd + 1) * LANES] = oh.astype(o_ref.dtype)


def _gdn_chunks(q, k, v, gate, aux, s0, out_gain, b, t, chunk, nb):
    n, d = q.shape
    nc = t // chunk
    seq = lambda a: a.reshape(b, t, a.shape[1])
    row = lambda bi, ci: (bi, ci, 0)
    st = lambda bi, ci: (bi, 0, 0, 0)
    sshape = (nb, GDN_HEADS, GDN_DH, GDN_DH)
    o, s_new = pl.pallas_call(
        functools.partial(_gdn_chunk_kernel, chunk=chunk),
        grid=(b // nb, nc),
        in_specs=[pl.BlockSpec((nb, chunk, d), row)] * 4 + [pl.BlockSpec((nb, chunk, LANES), row),
                                                           pl.BlockSpec(sshape, st),
                                                           pl.BlockSpec((1, GDN_DH), lambda bi, ci: (0, 0))],
        out_specs=[pl.BlockSpec((nb, chunk, d), row), pl.BlockSpec(sshape, st)],
        out_shape=[jax.ShapeDtypeStruct((b, t, d), BF16), jax.ShapeDtypeStruct(s0.shape, F32)],
        compiler_params=_params("parallel", "arbitrary"),
        name="gdn_chunks",
    )(seq(q), seq(k), seq(v), seq(gate), seq(aux), s0, out_gain[None, :])
    return o.reshape(n, d), s_new


def _router_kernel(x_ref, o_ref, wo_ref, g_ref, r_ref, x1_ref, h_ref, comb_ref):
    x1 = x_ref[...] + jnp.dot(o_ref[...], wo_ref[...], preferred_element_type=F32)
    x1_ref[...] = x1
    h = _rms(x1, g_ref[...])
    h_ref[...] = h.astype(h_ref.dtype)
    logits = jnp.dot(h, r_ref[...], preferred_element_type=F32, precision=HIGHEST)
    lane = lax.broadcasted_iota(jnp.int32, logits.shape, 1)
    logits = jnp.where(lane < N_EXPERTS, logits, -jnp.inf)
    m1 = jnp.max(logits, axis=1, keepdims=True)
    i1 = jnp.min(jnp.where(logits == m1, lane, LANES), axis=1, keepdims=True)
    rest = jnp.where(lane == i1, -jnp.inf, logits)
    m2 = jnp.max(rest, axis=1, keepdims=True)
    i2 = jnp.min(jnp.where(rest == m2, lane, LANES), axis=1, keepdims=True)
    e2 = jnp.exp(m2 - m1)
    g1 = 1.0 / (1.0 + e2)
    g2 = e2 / (1.0 + e2)
    comb_ref[...] = jnp.where(lane == i1, g1, jnp.where(lane == i2, g2, 0.0))


def _moe_router(x, o, w_o, norm_g, router, tm):
    n, d = x.shape
    r_pad = jnp.concatenate([router, jnp.zeros((d, LANES - N_EXPERTS), F32)], axis=1)
    row = lambda i: (i, 0)
    const = lambda i: (0, 0)
    return pl.pallas_call(
        _router_kernel,
        grid=(n // tm,),
        in_specs=[pl.BlockSpec((tm, d), row), pl.BlockSpec((tm, d), row), pl.BlockSpec((d, d), const),
                  pl.BlockSpec((1, d), const), pl.BlockSpec((d, LANES), const)],
        out_specs=[pl.BlockSpec((tm, d), row), pl.BlockSpec((tm, d), row), pl.BlockSpec((tm, LANES), row)],
        out_shape=[jax.ShapeDtypeStruct((n, d), F32), jax.ShapeDtypeStruct((n, d), BF16),
                   jax.ShapeDtypeStruct((n, LANES), F32)],
        compiler_params=_params("parallel"),
        name="moe_router",
    )(x, o, w_o.astype(BF16), norm_g[None, :], r_pad)


def _moe_dense_kernel(x1_ref, h_ref, comb_ref, w1_ref, w3_ref, w2_ref, y_ref):
    e = pl.program_id(1)
    j = pl.program_id(2)

    @pl.when((e == 0) & (j == 0))
    def _():
        y_ref[...] = x1_ref[...]

    comb = comb_ref[...]
    lane = lax.broadcasted_iota(jnp.int32, comb.shape, 1)
    ce = jnp.sum(jnp.where(lane == e, comb, 0.0), axis=1, keepdims=True)
    h = h_ref[...]
    a = jnp.dot(h, w1_ref[0], preferred_element_type=F32)
    b = jnp.dot(h, w3_ref[0], preferred_element_type=F32)
    t = (_silu(a) * b).astype(BF16)
    y_ref[...] += ce * jnp.dot(t, w2_ref[0], preferred_element_type=F32)


def _moe_dense(x1, h, comb, w1, w3, w2, tm, tf):
    n, d = x1.shape
    ne, _, dff = w1.shape
    row = lambda i, e, j: (i, 0)
    return pl.pallas_call(
        _moe_dense_kernel,
        grid=(n // tm, ne, dff // tf),
        in_specs=[pl.BlockSpec((tm, d), row), pl.BlockSpec((tm, d), row), pl.BlockSpec((tm, LANES), row),
                  pl.BlockSpec((1, d, tf), lambda i, e, j: (e, 0, j)),
                  pl.BlockSpec((1, d, tf), lambda i, e, j: (e, 0, j)),
                  pl.BlockSpec((1, tf, d), lambda i, e, j: (e, j, 0))],
        out_specs=pl.BlockSpec((tm, d), row),
        out_shape=jax.ShapeDtypeStruct((n, d), F32),
        compiler_params=_params("parallel", "arbitrary", "arbitrary"),
        name="moe_dense",
    )(x1, h, comb, w1.astype(BF16), w3.astype(BF16), w2.astype(BF16))


def _fox_layer(x, b, t, cache, w, ffn_w):
    n, d = x.shape
    tm = min(512, n)
    q, k, v, logf = _fox_proj(x, *w[:6], tm=tm)
    if cache is None:
        c = _cumsum_time(logf.reshape(b, t, FOX_HEADS))
        o = _fox_attn_prompt(q, k, v, c, b, t, tq=min(512, t), tk=min(512, t))
    else:
        ck, cv, cl = cache
        past = ck.shape[1]
        c = _cumsum_time(jnp.concatenate([cl, logf.reshape(b, t, FOX_HEADS)], axis=1))
        o = _fox_attn_sample(q, k, v, ck.reshape(b, past, d), cv.reshape(b, past, d), c, b, t)
    y = _attn_out_ffn(x, o, w[6], *ffn_w, tm=tm, tf=512)
    return y, k, v, logf


def _gdn_layer(x, b, t, chunk, hist, s0, w, moe_w):
    n, d = x.shape
    norm_g, w_qkv, conv_w, w_beta, w_a, a_log, dt_bias, w_gate, out_gain, w_o = w
    tm = min(512, t)
    q, k, v, gate, aux, new_hist = _gdn_proj(x, hist, b, t, tm, norm_g, w_qkv, conv_w, w_beta, w_a,
                                             a_log, dt_bias, w_gate)
    groups_per_seq = GDN_HEADS // (LANES // chunk)
    nb = max(1, min(b, GDN_CHAINS_PER_STEP // groups_per_seq))
    o, s_new = _gdn_chunks(q, k, v, gate, aux, s0, out_gain, b, t, chunk, nb)
    moe_norm, router, w1, w3, w2 = moe_w
    x1, h, comb = _moe_router(x, o, w_o, moe_norm, router, tm=min(512, n))
    y = _moe_dense(x1, h, comb, w1, w3, w2, tm=min(512, n), tf=512)
    return y, s_new, new_hist


def kernel(x_prompt, x_sample, cache_fox_k, cache_fox_v, cache_fox_logf, state_gdn_s, state_gdn_conv, fox_norm, fox_w_qkv, fox_q_gain, fox_k_gain, fox_w_f, fox_b_f, fox_w_o, ffn_norm, ffn_w1, ffn_w3, ffn_w2, gdn_norm, gdn_w_qkv, gdn_conv_w, gdn_w_beta, gdn_w_a, gdn_a_log, gdn_dt_bias, gdn_w_gate, gdn_out_gain, gdn_w_o, moe_norm, moe_router, moe_w1, moe_w3, moe_w2):
    bp, tp, d = x_prompt.shape
    bs, ts, _ = x_sample.shape
    depth = fox_norm.shape[0] + gdn_norm.shape[0]
    cdim = gdn_w_qkv.shape[2]
    yp = x_prompt.reshape(bp * tp, d)
    ys = x_sample.reshape(bs * ts, d)
    kp_l, vp_l, lp_l, ks_l, vs_l, ls_l = [], [], [], [], [], []
    sp_l, cp_l, ss_l, cs_l = [], [], [], []
    for i in range(depth):
        j = i // 2
        if i % 2 == 0:
            fw = (fox_norm[j], fox_w_qkv[j], fox_q_gain[j], fox_k_gain[j], fox_w_f[j], fox_b_f[j], fox_w_o[j])
            ffw = (ffn_norm[j], ffn_w1[j], ffn_w3[j], ffn_w2[j])
            yp, kp, vp, lp = _fox_layer(yp, bp, tp, None, fw, ffw)
            ys, ks, vs, ls = _fox_layer(ys, bs, ts, (cache_fox_k[j], cache_fox_v[j], cache_fox_logf[j]), fw, ffw)
            kp_l.append(kp.reshape(bp, tp, FOX_HEADS, FOX_DH)); vp_l.append(vp.reshape(bp, tp, FOX_HEADS, FOX_DH))
            lp_l.append(lp.reshape(bp, tp, FOX_HEADS))
            ks_l.append(ks.reshape(bs, ts, FOX_HEADS, FOX_DH)); vs_l.append(vs.reshape(bs, ts, FOX_HEADS, FOX_DH))
            ls_l.append(ls.reshape(bs, ts, FOX_HEADS))
        else:
            gw = (gdn_norm[j], gdn_w_qkv[j], gdn_conv_w[j], gdn_w_beta[j], gdn_w_a[j], gdn_a_log[j],
                  gdn_dt_bias[j], gdn_w_gate[j], gdn_out_gain[j], gdn_w_o[j])
            mw = (moe_norm[j], moe_router[j], moe_w1[j], moe_w3[j], moe_w2[j])
            hist0 = jnp.zeros((bp, GDN_CONV - 1, cdim), F32)
            s_zero = jnp.zeros((bp, GDN_HEADS, GDN_DH, GDN_DH), F32)
            yp, sp, cp = _gdn_layer(yp, bp, tp, 64, hist0, s_zero, gw, mw)
            ys, ss, cs = _gdn_layer(ys, bs, ts, ts, state_gdn_conv[j], state_gdn_s[j], gw, mw)
            sp_l.append(sp); cp_l.append(cp); ss_l.append(ss); cs_l.append(cs)
    return (yp.reshape(bp, tp, d), ys.reshape(bs, ts, d),
            jnp.stack(kp_l), jnp.stack(vp_l), jnp.stack(lp_l),
            jnp.stack(ks_l), jnp.stack(vs_l), jnp.stack(ls_l),
            jnp.stack(sp_l), jnp.stack(cp_l), jnp.stack(ss_l), jnp.stack(cs_l))
```

```python
import functools

import jax
import jax.numpy as jnp
from jax import lax
from jax.experimental import pallas as pl
from jax.experimental.pallas import tpu as pltpu

F32 = jnp.float32
BF16 = jnp.bfloat16
EPS = 1e-6
LANES = 128
VMEM_LIMIT = 56 * 1024 * 1024

FOX_HEADS = 16
FOX_DH = 64
GDN_HEADS = 8
GDN_DH = 128
GDN_CONV = 4
N_EXPERTS = 8
MOE_TILE = 1024
MOE_TF = 512
GDN_CHAINS_PER_STEP = 16
HIGHEST = lax.Precision.HIGHEST


def _params(*sem):
    return pltpu.CompilerParams(dimension_semantics=sem, vmem_limit_bytes=VMEM_LIMIT)


def _rms(x, gain):
    return x * lax.rsqrt(jnp.mean(x * x, axis=-1, keepdims=True) + EPS) * gain


def _silu(x):
    return x * (1.0 / (1.0 + jnp.exp(-x)))


def _softplus(x):
    return jnp.maximum(x, 0.0) + jnp.log(1.0 + jnp.exp(-jnp.abs(x)))


def _split3(c):
    hi = c.astype(BF16).astype(F32)
    r = c - hi
    mid = r.astype(BF16).astype(F32)
    lo = (r - mid).astype(BF16).astype(F32)
    return hi, mid, lo


def _fox_proj_kernel(x_ref, g_ref, w_ref, qg_ref, kg_ref, bf_ref, gm_ref, gmt_ref,
                     q_ref, k_ref, v_ref, lf_ref):
    d = x_ref.shape[1]
    h = _rms(x_ref[...], g_ref[...])
    y = jnp.dot(h.astype(BF16), w_ref[...], preferred_element_type=F32)

    def headnorm(z, gain):
        ss = jnp.dot((z * z).astype(BF16), gm_ref[...], preferred_element_type=F32)
        inv = lax.rsqrt(ss * (1.0 / FOX_DH) + EPS)
        hi = inv.astype(BF16)
        lo = (inv - hi.astype(F32)).astype(BF16)
        full = (jnp.dot(hi, gmt_ref[...], preferred_element_type=F32)
                + jnp.dot(lo, gmt_ref[...], preferred_element_type=F32))
        return z * full * gain

    q_ref[...] = (headnorm(y[:, :d], qg_ref[...]) * (FOX_DH ** -0.5)).astype(q_ref.dtype)
    k_ref[...] = headnorm(y[:, d:2 * d], kg_ref[...])
    v_ref[...] = y[:, 2 * d:3 * d]
    z = y[:, 3 * d:3 * d + LANES] + bf_ref[...]
    lf = jnp.minimum(z, 0.0) - jnp.log(1.0 + jnp.exp(-jnp.abs(z)))
    lf_ref[...] = lf[:, :FOX_HEADS]


def _fox_proj(x, norm_g, w_qkv, q_gain, k_gain, w_f, b_f, tm):
    n, d = x.shape
    w_all = jnp.concatenate(
        [w_qkv, w_f, jnp.zeros((d, LANES - FOX_HEADS), F32)], axis=1).astype(BF16)
    bf = jnp.concatenate([b_f, jnp.zeros((LANES - FOX_HEADS,), F32)])[None, :]
    head_of_col = jnp.arange(d) // FOX_DH
    gm = (head_of_col[:, None] == jnp.arange(LANES)[None, :]).astype(BF16)
    gmt = gm.T
    qg = jnp.tile(q_gain, FOX_HEADS)[None, :]
    kg = jnp.tile(k_gain, FOX_HEADS)[None, :]
    const = lambda i: (0, 0)
    row = lambda i: (i, 0)
    return pl.pallas_call(
        _fox_proj_kernel,
        grid=(n // tm,),
        in_specs=[pl.BlockSpec((tm, d), row), pl.BlockSpec((1, d), const),
                  pl.BlockSpec((d, 3 * d + LANES), const), pl.BlockSpec((1, d), const),
                  pl.BlockSpec((1, d), const), pl.BlockSpec((1, LANES), const),
                  pl.BlockSpec((d, LANES), const), pl.BlockSpec((LANES, d), const)],
        out_specs=[pl.BlockSpec((tm, d), row), pl.BlockSpec((tm, d), row), pl.BlockSpec((tm, d), row),
                   pl.BlockSpec((tm, FOX_HEADS), row)],
        out_shape=[jax.ShapeDtypeStruct((n, d), BF16), jax.ShapeDtypeStruct((n, d), F32),
                   jax.ShapeDtypeStruct((n, d), F32), jax.ShapeDtypeStruct((n, FOX_HEADS), F32)],
        compiler_params=_params("parallel"),
        name="fox_proj",
    )(x, norm_g[None, :], w_all, qg, kg, bf, gm, gmt)


def _cumsum_kernel(x_ref, o_ref):
    x = x_ref[0]
    t = x.shape[0]
    row = lax.broadcasted_iota(jnp.int32, x.shape, 0)
    s = 1
    while s < t:
        x = x + jnp.where(row >= s, pltpu.roll(x, s, axis=0), 0.0)
        s *= 2
    o_ref[0] = x


def _cumsum_time(x):
    b, t, hn = x.shape
    spec = pl.BlockSpec((1, t, hn), lambda i: (i, 0, 0))
    return pl.pallas_call(
        _cumsum_kernel, grid=(b,), in_specs=[spec], out_specs=spec,
        out_shape=jax.ShapeDtypeStruct(x.shape, F32),
        compiler_params=_params("parallel"), name="logf_cumsum",
    )(x)


def _head_col(c16, head):
    lane = lax.broadcasted_iota(jnp.int32, c16.shape, 1)
    return jnp.sum(jnp.where(lane == head, c16, 0.0), axis=1, keepdims=True)


def _aug(data, ccol, hh, is_q):
    lane = lax.broadcasted_iota(jnp.int32, data.shape, 1)
    base = FOX_DH if hh == 0 else 0
    in_data = (lane < FOX_DH) if hh == 0 else (lane >= FOX_DH)
    hi, mid, lo = _split3(ccol)
    lane1 = lax.broadcasted_iota(jnp.int32, (1, data.shape[1]), 1)
    if is_q:
        pieces = {0: hi, 1: mid, 2: lo}
        ones = (lane1 >= base + 3) & (lane1 < base + 6)
    else:
        pieces = {3: -hi, 4: -mid, 5: -lo}
        ones = (lane1 >= base) & (lane1 < base + 3)
    aug = jnp.where(ones, 1.0, 0.0)
    for i, p in pieces.items():
        aug = jnp.where(lane == base + i, p, aug)
    return jnp.where(in_data, data, aug)


def _v_aug(v2, hh):
    lane = lax.broadcasted_iota(jnp.int32, v2.shape, 1)
    in_data = (lane < FOX_DH) if hh == 0 else (lane >= FOX_DH)
    return jnp.where(in_data, v2, 1.0)


def _finish_heads(acc0, acc1):
    lane = lax.broadcasted_iota(jnp.int32, acc0.shape, 1)
    o0 = acc0 / pltpu.roll(acc0, FOX_DH, axis=1)
    o1 = acc1 / pltpu.roll(acc1, FOX_DH, axis=1)
    return jnp.where(lane < FOX_DH, o0, o1)


def _fox_attn_kernel(q_ref, k_ref, v_ref, c_ref, o_ref, kt_s, v_s, *, tq, tk):
    hp = pl.program_id(1)
    t = k_ref.shape[0]
    nkb = t // tk
    nd = tq // tk
    c16 = c_ref[0]
    k2 = k_ref[...]
    v2 = v_ref[...]
    ccols = [_head_col(c16, 2 * hp + hh) for hh in range(2)]
    for hh in range(2):
        kaug = _aug(k2, ccols[hh], hh, False)
        for jb in range(nkb):
            kt_s[hh, jb] = kaug[jb * tk:(jb + 1) * tk, :].T.astype(BF16)
        v_s[hh] = _v_aug(v2, hh).astype(BF16)

    rows = lax.broadcasted_iota(jnp.int32, (tq, tk), 0)
    cols = lax.broadcasted_iota(jnp.int32, (tq, tk), 1)
    chains = []
    for i in range(t // tq):
        q2 = q_ref[i * tq:(i + 1) * tq, :].astype(F32)
        for hh in range(2):
            qb = _aug(q2, ccols[hh][i * tq:(i + 1) * tq], hh, True).astype(BF16)
            order = list(range(i * nd, (i + 1) * nd)) + list(range(i * nd))
            chains.append(dict(i=i, hh=hh, qb=qb, order=order, m=None, acc=None))
    for rnd in range(max(len(c["order"]) for c in chains)):
        live = [c for c in chains if rnd < len(c["order"])]
        for c in live:
            j = c["order"][rnd]
            s = jnp.dot(c["qb"], kt_s[c["hh"], j], preferred_element_type=F32)
            dblk = j - c["i"] * nd
            if dblk >= 0:
                s = jnp.where(cols + dblk * tk <= rows, s, -jnp.inf)
            c["s"] = s
        for c in live:
            bm = jnp.max(c["s"], axis=1, keepdims=True)
            m_new = bm if c["m"] is None else jnp.maximum(c["m"], bm)
            c["p"] = jnp.exp(c["s"] - m_new).astype(BF16)
            c["alpha"] = None if c["m"] is None else jnp.exp(c["m"] - m_new)
            c["m"] = m_new
        for c in live:
            j = c["order"][rnd]
            pv = jnp.dot(c["p"], v_s[c["hh"], j * tk:(j + 1) * tk, :], preferred_element_type=F32)
            c["acc"] = pv if c["acc"] is None else c["alpha"] * c["acc"] + pv
    for i in range(t // tq):
        o_ref[i * tq:(i + 1) * tq, :] = _finish_heads(chains[2 * i]["acc"],
                                                       chains[2 * i + 1]["acc"]).astype(o_ref.dtype)


def _fox_attn_prompt(q, k, v, c, b, t, tq, tk):
    n, d = q.shape
    nhp = d // LANES
    blk = pl.BlockSpec((t, LANES), lambda bi, hp: (bi, hp))
    return pl.pallas_call(
        functools.partial(_fox_attn_kernel, tq=tq, tk=tk),
        grid=(b, nhp),
        in_specs=[blk, blk, blk, pl.BlockSpec((1, t, FOX_HEADS), lambda bi, hp: (bi, 0, 0))],
        out_specs=blk,
        out_shape=jax.ShapeDtypeStruct((n, d), BF16),
        scratch_shapes=[pltpu.VMEM((2, t // tk, LANES, tk), BF16), pltpu.VMEM((2, t, LANES), BF16)],
        compiler_params=_params("parallel", "parallel"),
        name="fox_attn_prompt",
    )(q, k, v, c)


def _fox_attn_sample_kernel(q_ref, kn_ref, vn_ref, kc_ref, vc_ref, c_ref, o_ref):
    hp = pl.program_id(1)
    t = q_ref.shape[0]
    past = kc_ref.shape[0]
    c16 = c_ref[0]
    q2 = q_ref[...].astype(F32)
    rows = lax.broadcasted_iota(jnp.int32, (t, t), 0)
    cols = lax.broadcasted_iota(jnp.int32, (t, t), 1)
    nt = (((1,), (1,)), ((), ()))
    accs = []
    for hh in range(2):
        ccol = _head_col(c16, 2 * hp + hh)
        qaug = _aug(q2, ccol[past:], hh, True).astype(BF16)
        kc = _aug(kc_ref[...], ccol[:past], hh, False).astype(BF16)
        kn = _aug(kn_ref[...], ccol[past:], hh, False).astype(BF16)
        s_c = lax.dot_general(qaug, kc, nt, preferred_element_type=F32)
        s_n = lax.dot_general(qaug, kn, nt, preferred_element_type=F32)
        s_n = jnp.where(cols <= rows, s_n, -jnp.inf)
        m = jnp.maximum(jnp.max(s_c, axis=1, keepdims=True), jnp.max(s_n, axis=1, keepdims=True))
        p_c = jnp.exp(s_c - m).astype(BF16)
        p_n = jnp.exp(s_n - m).astype(BF16)
        acc = (jnp.dot(p_c, _v_aug(vc_ref[...], hh).astype(BF16), preferred_element_type=F32)
               + jnp.dot(p_n, _v_aug(vn_ref[...], hh).astype(BF16), preferred_element_type=F32))
        accs.append(acc)
    o_ref[...] = _finish_heads(accs[0], accs[1]).astype(o_ref.dtype)


def _fox_attn_sample(q, k_new, v_new, cache_k, cache_v, c_all, b, t):
    n, d = q.shape
    past = cache_k.shape[1]
    nhp = d // LANES
    new_spec = pl.BlockSpec((t, LANES), lambda bi, hp: (bi, hp))
    cache_spec = pl.BlockSpec((None, past, LANES), lambda bi, hp: (bi, 0, hp))
    return pl.pallas_call(
        _fox_attn_sample_kernel,
        grid=(b, nhp),
        in_specs=[new_spec, new_spec, new_spec, cache_spec, cache_spec,
                  pl.BlockSpec((1, past + t, FOX_HEADS), lambda bi, hp: (bi, 0, 0))],
        out_specs=new_spec,
        out_shape=jax.ShapeDtypeStruct((n, d), BF16),
        compiler_params=_params("parallel", "parallel"),
        name="fox_attn_sample",
    )(q, k_new, v_new, cache_k, cache_v, c_all)


def _ffn_kernel(x_ref, o_ref, wo_ref, g_ref, w1_ref, w3_ref, w2_ref, y_ref, h_s):
    j = pl.program_id(1)

    @pl.when(j == 0)
    def _():
        x1 = x_ref[...] + jnp.dot(o_ref[...], wo_ref[...], preferred_element_type=F32)
        y_ref[...] = x1
        h_s[...] = _rms(x1, g_ref[...]).astype(BF16)

    h = h_s[...]
    a = jnp.dot(h, w1_ref[...], preferred_element_type=F32)
    b = jnp.dot(h, w3_ref[...], preferred_element_type=F32)
    t = (_silu(a) * b).astype(BF16)
    y_ref[...] += jnp.dot(t, w2_ref[...], preferred_element_type=F32)


def _attn_out_ffn(x, o, w_o, norm_g, w1, w3, w2, tm, tf):
    n, d = x.shape
    dff = w1.shape[1]
    row = lambda i, j: (i, 0)
    const = lambda i, j: (0, 0)
    return pl.pallas_call(
        _ffn_kernel,
        grid=(n // tm, dff // tf),
        in_specs=[pl.BlockSpec((tm, d), row), pl.BlockSpec((tm, d), row), pl.BlockSpec((d, d), const),
                  pl.BlockSpec((1, d), const), pl.BlockSpec((d, tf), lambda i, j: (0, j)),
                  pl.BlockSpec((d, tf), lambda i, j: (0, j)), pl.BlockSpec((tf, d), lambda i, j: (j, 0))],
        out_specs=pl.BlockSpec((tm, d), row),
        out_shape=jax.ShapeDtypeStruct((n, d), F32),
        scratch_shapes=[pltpu.VMEM((tm, d), BF16)],
        compiler_params=_params("parallel", "arbitrary"),
        name="attn_out_ffn",
    )(x, o, w_o.astype(BF16), norm_g[None, :], w1.astype(BF16), w3.astype(BF16), w2.astype(BF16))


def _gdn_proj_kernel(x_ref, g_ref, w_ref, cw_ref, hist_ref, alog_ref, dtb_ref,
                     q_ref, k_ref, v_ref, gate_ref, aux_ref, hout_ref, pad_s, *, steps_per_seq):
    i = pl.program_id(0)
    tm, d = x_ref.shape
    cdim = 3 * d
    first = (i % steps_per_seq) == 0

    @pl.when(first)
    def _():
        pad_s[0:8, :] = hist_ref[0]

    @pl.when(jnp.logical_not(first))
    def _():
        pad_s[0:8, :] = pad_s[tm:tm + 8, :]

    h = _rms(x_ref[...], g_ref[...])
    y = jnp.dot(h.astype(BF16), w_ref[...], preferred_element_type=F32)
    pad_s[8:8 + tm, :] = y[:, :cdim]
    hout_ref[0] = pad_s[tm:tm + 8, :]

    for blk in range(cdim // LANES):
        cs = slice(blk * LANES, (blk + 1) * LANES)
        mixed = jnp.zeros((tm, LANES), F32)
        for w in range(GDN_CONV):
            mixed = mixed + cw_ref[w:w + 1, cs] * pad_s[8 - (GDN_CONV - 1) + w:8 - (GDN_CONV - 1) + w + tm, cs]
        mixed = _silu(mixed)
        part, hd = divmod(blk, GDN_HEADS)
        hs = slice(hd * LANES, (hd + 1) * LANES)
        if part == 2:
            v_ref[:, hs] = mixed
        else:
            nrm = mixed * lax.rsqrt(jnp.sum(mixed * mixed, axis=1, keepdims=True) + EPS)
            if part == 0:
                q_ref[:, hs] = nrm * (GDN_DH ** -0.5)
            else:
                k_ref[:, hs] = nrm

    gate_ref[...] = _silu(y[:, cdim:cdim + d])
    z = y[:, cdim + d:cdim + d + LANES]
    lane = lax.broadcasted_iota(jnp.int32, z.shape, 1)
    beta = 1.0 / (1.0 + jnp.exp(-z))
    gdec = -jnp.exp(alog_ref[...]) * _softplus(z + dtb_ref[...])
    aux_ref[...] = jnp.where(lane < GDN_HEADS, beta, gdec)


def _gdn_proj(x, hist, b, t, tm, norm_g, w_qkv, conv_w, w_beta, w_a, a_log, dt_bias, w_gate):
    n, d = x.shape
    cdim = 3 * d
    w_all = jnp.concatenate(
        [w_qkv, w_gate, w_beta, w_a, jnp.zeros((d, LANES - 2 * GDN_HEADS), F32)], axis=1).astype(BF16)
    hist8 = jnp.concatenate([jnp.zeros((b, 8 - (GDN_CONV - 1), cdim), F32), hist], axis=1)
    lane_vec = lambda a: jnp.concatenate(
        [jnp.zeros((GDN_HEADS,), F32), a, jnp.zeros((LANES - 2 * GDN_HEADS,), F32)])[None, :]
    steps = t // tm
    row = lambda i: (i, 0)
    const = lambda i: (0, 0)
    outs = pl.pallas_call(
        functools.partial(_gdn_proj_kernel, steps_per_seq=steps),
        grid=(n // tm,),
        in_specs=[pl.BlockSpec((tm, d), row), pl.BlockSpec((1, d), const),
                  pl.BlockSpec((d, 4 * d + LANES), const), pl.BlockSpec((GDN_CONV, cdim), const),
                  pl.BlockSpec((1, 8, cdim), lambda i: (i // steps, 0, 0)),
                  pl.BlockSpec((1, LANES), const), pl.BlockSpec((1, LANES), const)],
        out_specs=[pl.BlockSpec((tm, d), row), pl.BlockSpec((tm, d), row), pl.BlockSpec((tm, d), row),
                   pl.BlockSpec((tm, d), row), pl.BlockSpec((tm, LANES), row),
                   pl.BlockSpec((1, 8, cdim), lambda i: (i // steps, 0, 0))],
        out_shape=[jax.ShapeDtypeStruct((n, d), F32)] * 4 + [jax.ShapeDtypeStruct((n, LANES), F32),
                                                           jax.ShapeDtypeStruct((b, 8, cdim), F32)],
        scratch_shapes=[pltpu.VMEM((tm + 8, cdim), F32)],
        compiler_params=_params("arbitrary"),
        name="gdn_proj",
    )(x, norm_g[None, :], w_all, conv_w, hist8, lane_vec(a_log), lane_vec(dt_bias))
    q, k, v, gate, aux, hout = outs
    return q, k, v, gate, aux, hout[:, 8 - (GDN_CONV - 1):, :]


def _gdn_chunk_kernel(q_ref, k_ref, v_ref, gate_ref, aux_ref, s0_ref, og_ref, o_ref, s_ref, *, chunk):
    ci = pl.program_id(1)
    nb = q_ref.shape[0]
    per = LANES // chunk
    ngroups = GDN_HEADS // per
    nsq = chunk.bit_length() - 2

    @pl.when(ci == 0)
    def _():
        s_ref[...] = s0_ref[...]

    bi = lax.broadcasted_iota(jnp.int32, (LANES, LANES), 0)
    bj = lax.broadcasted_iota(jnp.int32, (LANES, LANES), 1)
    same = (bi // chunk) == (bj // chunk)
    causal = same & (bi >= bj)
    strict = same & (bi > bj)
    eye = (bi == bj).astype(F32)
    tcol = bj // chunk
    nt = (((1,), (1,)), ((), ()))

    chains = []
    for bb in range(nb):
        aux = aux_ref[bb]
        trow = lax.broadcasted_iota(jnp.int32, aux.shape, 0)
        gc = aux
        sh = 1
        while sh < chunk:
            gc = gc + jnp.where(trow >= sh, pltpu.roll(gc, sh, axis=0), 0.0)
            sh *= 2
        glast = gc[chunk - 1:chunk, :]
        for grp in range(ngroups):
            heads = [grp * per + a for a in range(per)]
            stack = lambda ref: jnp.concatenate(
                [ref[bb, :, hd * LANES:(hd + 1) * LANES] for hd in heads], axis=0)
            col = lambda arr, off: jnp.concatenate([arr[:, off + hd:off + hd + 1] for hd in heads], axis=0)
            c = dict(bb=bb, heads=heads, glast=glast)
            c["q2"], c["k2"], c["v2"] = stack(q_ref), stack(k_ref), stack(v_ref)
            c["beta"] = col(aux, 0)
            gcol = col(gc, GDN_HEADS)
            gl = jnp.concatenate([jnp.broadcast_to(glast[:, GDN_HEADS + hd:GDN_HEADS + hd + 1], (chunk, 1))
                                  for hd in heads], axis=0)
            grow = jnp.broadcast_to(gcol, (LANES, LANES)).T
            c["decay"] = jnp.exp(jnp.where(causal, gcol - grow, -jnp.inf))
            c["egc"] = jnp.exp(gcol)
            c["kdec"] = jnp.exp(gl - gcol)
            chains.append(c)

    for c in chains:
        k2b = c["k2"].astype(BF16)
        c["both"] = lax.dot_general(jnp.concatenate([k2b, c["q2"].astype(BF16)], axis=0), k2b, nt,
                                    preferred_element_type=F32)
    for c in chains:
        c["x"] = -jnp.where(strict, c["beta"] * c["both"][:LANES] * c["decay"], 0.0)
        c["attn"] = (c["both"][LANES:] * c["decay"]).astype(BF16)
        xb = c["x"].astype(BF16)
        c["p"] = jnp.dot(xb, xb, preferred_element_type=F32)
        c["tser"] = eye + c["x"]
    for it in range(nsq):
        for c in chains:
            c["pb"] = c["p"].astype(BF16)
            c["tser"] = c["tser"] + jnp.dot(c["tser"].astype(BF16), c["pb"], preferred_element_type=F32)
        if it + 1 < nsq:
            for c in chains:
                c["p"] = jnp.dot(c["pb"], c["pb"], preferred_element_type=F32)
    for c in chains:
        rhs = jnp.concatenate([c["v2"] * c["beta"], c["k2"] * (c["beta"] * c["egc"])], axis=1)
        c["sol"] = rhs + jnp.dot((c["tser"] - eye).astype(BF16), rhs.astype(BF16), preferred_element_type=F32)
    for c in chains:
        w2b = c["sol"][:, LANES:].astype(BF16)
        qd = (c["q2"] * c["egc"]).astype(BF16)
        c["wq"] = []
        for a, hd in enumerate(c["heads"]):
            rs = slice(a * chunk, (a + 1) * chunk)
            c["wq"].append(jnp.dot(jnp.concatenate([w2b[rs], qd[rs]], axis=0), s_ref[c["bb"], hd].astype(BF16),
                                   preferred_element_type=F32))
    for c in chains:
        ws = jnp.concatenate([wq[:chunk] for wq in c["wq"]], axis=0)
        c["qs"] = jnp.concatenate([wq[chunk:] for wq in c["wq"]], axis=0)
        c["vnb"] = (c["sol"][:, :LANES] - ws).astype(BF16)
        c["kdt"] = (c["k2"] * c["kdec"]).T
    for c in chains:
        c["o2"] = c["qs"] + jnp.dot(c["attn"], c["vnb"], preferred_element_type=F32)
        for a, hd in enumerate(c["heads"]):
            kdt_h = jnp.where(tcol == a, c["kdt"], 0.0).astype(BF16)
            decay_s = jnp.exp(c["glast"][:, GDN_HEADS + hd:GDN_HEADS + hd + 1])
            s_ref[c["bb"], hd] = (s_ref[c["bb"], hd] * decay_s
                                  + jnp.dot(kdt_h, c["vnb"], preferred_element_type=F32))
    for c in chains:
        for a, hd in enumerate(c["heads"]):
            rs = slice(a * chunk, (a + 1) * chunk)
            oh = _rms(c["o2"][rs], og_ref[...]) * gate_ref[c["bb"], :, hd * LANES:(hd + 1) * LANES]
            o_ref[c["bb"], :, hd * LANES:(hd + 1) * LANES] = oh.astype(o_ref.dtype)


def _gdn_chunks(q, k, v, gate, aux, s0, out_gain, b, t, chunk, nb):
    n, d = q.shape
    nc = t // chunk
    seq = lambda a: a.reshape(b, t, a.shape[1])
    row = lambda bi, ci: (bi, ci, 0)
    st = lambda bi, ci: (bi, 0, 0, 0)
    sshape = (nb, GDN_HEADS, GDN_DH, GDN_DH)
    o, s_new = pl.pallas_call(
        functools.partial(_gdn_chunk_kernel, chunk=chunk),
        grid=(b // nb, nc),
        in_specs=[pl.BlockSpec((nb, chunk, d), row)] * 4 + [pl.BlockSpec((nb, chunk, LANES), row),
                                                           pl.BlockSpec(sshape, st),
                                                           pl.BlockSpec((1, GDN_DH), lambda bi, ci: (0, 0))],
        out_specs=[pl.BlockSpec((nb, chunk, d), row), pl.BlockSpec(sshape, st)],
        out_shape=[jax.ShapeDtypeStruct((b, t, d), BF16), jax.ShapeDtypeStruct(s0.shape, F32)],
        compiler_params=_params("parallel", "arbitrary"),
        name="gdn_chunks",
    )(seq(q), seq(k), seq(v), seq(gate), seq(aux), s0, out_gain[None, :])
    return o.reshape(n, d), s_new


def _router_kernel(x_ref, o_ref, wo_ref, g_ref, r_ref, base_ref, x1_ref, h3_ref, info_ref, cnt_ref, cnt_s,
                   *, cap):
    i = pl.program_id(0)
    tm, d = x_ref.shape

    @pl.when(i == 0)
    def _():
        cnt_s[...] = base_ref[...]

    x1 = x_ref[...] + jnp.dot(o_ref[...], wo_ref[...], preferred_element_type=F32)
    x1_ref[...] = x1
    h = _rms(x1, g_ref[...])
    nchunk = d // LANES
    for c in range(nchunk):
        h3_ref[pl.ds(c, tm, stride=nchunk), :] = h[:, c * LANES:(c + 1) * LANES]
    logits = jnp.dot(h, r_ref[...], preferred_element_type=F32, precision=HIGHEST)
    lane = lax.broadcasted_iota(jnp.int32, logits.shape, 1)
    logits = jnp.where(lane < N_EXPERTS, logits, -jnp.inf)
    m1 = jnp.max(logits, axis=1, keepdims=True)
    i1 = jnp.min(jnp.where(logits == m1, lane, LANES), axis=1, keepdims=True)
    rest = jnp.where(lane == i1, -jnp.inf, logits)
    m2 = jnp.max(rest, axis=1, keepdims=True)
    i2 = jnp.min(jnp.where(rest == m2, lane, LANES), axis=1, keepdims=True)
    e2 = jnp.exp(m2 - m1)
    g1 = 1.0 / (1.0 + e2)
    g2 = e2 / (1.0 + e2)
    member = ((lane == i1) | (lane == i2)).astype(F32)
    ri = lax.broadcasted_iota(jnp.int32, (tm, tm), 0)
    rj = lax.broadcasted_iota(jnp.int32, (tm, tm), 1)
    rank = jnp.dot((rj < ri).astype(BF16), member.astype(BF16), preferred_element_type=F32)
    slot = cnt_s[...] + rank + lane.astype(F32) * float(cap)
    d1 = jnp.sum(jnp.where(lane == i1, slot, 0.0), axis=1, keepdims=True)
    d2 = jnp.sum(jnp.where(lane == i2, slot, 0.0), axis=1, keepdims=True)
    cnt_s[...] += jnp.sum(member, axis=0, keepdims=True)
    cnt_ref[...] = cnt_s[...]
    info = jnp.zeros(logits.shape, F32)
    for k, val in enumerate((i1.astype(F32), i2.astype(F32), g1, g2, d1, d2)):
        info = jnp.where(lane == k, val, info)
    info_ref[...] = info


def _moe_router(x, o, w_o, norm_g, router, base_counts, tm, cap):
    n, d = x.shape
    r_pad = jnp.concatenate([router, jnp.zeros((d, LANES - N_EXPERTS), F32)], axis=1)
    row = lambda i: (i, 0)
    const = lambda i: (0, 0)
    x1, h3, info, counts = pl.pallas_call(
        functools.partial(_router_kernel, cap=cap),
        grid=(n // tm,),
        in_specs=[pl.BlockSpec((tm, d), row), pl.BlockSpec((tm, d), row), pl.BlockSpec((d, d), const),
                  pl.BlockSpec((1, d), const), pl.BlockSpec((d, LANES), const), pl.BlockSpec((1, LANES), const)],
        out_specs=[pl.BlockSpec((tm, d), row), pl.BlockSpec((tm * (d // LANES), LANES), row),
                   pl.BlockSpec((tm, LANES), row), pl.BlockSpec((1, LANES), const)],
        out_shape=[jax.ShapeDtypeStruct((n, d), F32), jax.ShapeDtypeStruct((n * (d // LANES), LANES), F32),
                   jax.ShapeDtypeStruct((n, LANES), F32), jax.ShapeDtypeStruct((1, LANES), F32)],
        scratch_shapes=[pltpu.VMEM((1, LANES), F32)],
        compiler_params=_params("arbitrary"),
        name="moe_router",
    )(x, o, w_o.astype(BF16), norm_g[None, :], r_pad, base_counts)
    return x1, h3.reshape(n, d // LANES, LANES), info, counts


def _row_dests(info, tm):
    n = info.shape[0]
    dest = info[:, 4:6].astype(jnp.int32)
    return dest.reshape(n // tm, tm, 2).transpose(0, 2, 1).reshape(n // tm, 1, 2 * tm)


def _dispatch_kernel(*refs, tm, final, tile):
    if final:
        dest_ref, pad_ref, h3_ref, zeros_ref, _, xs_ref, sem = refs
    else:
        dest_ref, h3_ref, xs_ref, sem = refs
    i = pl.program_id(0)

    def copies(r):
        src = h3_ref.at[i * tm + r]
        return (pltpu.make_async_copy(src, xs_ref.at[dest_ref[0, 0, r]], sem),
                pltpu.make_async_copy(src, xs_ref.at[dest_ref[0, 0, tm + r]], sem))

    def issue(r, carry):
        for cp in copies(r):
            cp.start()
        return carry

    def drain(r, carry):
        for cp in copies(r):
            cp.wait()
        return carry

    lax.fori_loop(0, tm, issue, 0)
    lax.fori_loop(0, tm, drain, 0)

    if final:
        @pl.when(i == pl.num_programs(0) - 1)
        def _():
            for e in range(N_EXPERTS):
                start = pad_ref[0, e]
                tail = pad_ref[1, e]
                for bit in reversed(range(tile.bit_length() - 1)):
                    size = 1 << bit
                    off = start + ((tail >> (bit + 1)) << (bit + 1))

                    @pl.when(((tail >> bit) & 1) == 1)
                    def _():
                        cp = pltpu.make_async_copy(zeros_ref.at[pl.ds(0, size)], xs_ref.at[pl.ds(off, size)], sem)
                        cp.start()
                        cp.wait()


def _moe_dispatch(h3, dest, tm, rows, xs=None, pad=None, tile=None):
    n, s, _ = h3.shape
    final = xs is not None
    smem = functools.partial(pl.BlockSpec, memory_space=pltpu.SMEM)
    anyspec = pl.BlockSpec(memory_space=pl.ANY)
    in_specs = [smem((1, 1, 2 * tm), lambda i: (i, 0, 0))]
    args = [dest]
    if final:
        in_specs.append(smem((2, N_EXPERTS), lambda i: (0, 0)))
        args.append(pad)
    in_specs.append(anyspec)
    args.append(h3)
    if final:
        in_specs += [anyspec, anyspec]
        args += [jnp.zeros((tile // 2, s, LANES), F32), xs]
    return pl.pallas_call(
        functools.partial(_dispatch_kernel, tm=tm, final=final, tile=tile),
        grid=(n // tm,),
        in_specs=in_specs,
        out_specs=anyspec,
        out_shape=jax.ShapeDtypeStruct((rows, s, LANES), F32),
        scratch_shapes=[pltpu.SemaphoreType.DMA(())],
        input_output_aliases={len(args) - 1: 0} if final else {},
        compiler_params=_params("arbitrary"),
        name="moe_dispatch",
    )(*args)


def _experts_kernel(blk_ref, exp_ref, valid_ref, xs_ref, w1_ref, w3_ref, w2_ref, ys_ref, h_s, acc_s):
    t = pl.program_id(0)
    j = pl.program_id(1)
    tile, d = h_s.shape
    nchunk = d // LANES

    @pl.when(valid_ref[t] == 1)
    def _():
        @pl.when(j == 0)
        def _():
            for c in range(nchunk):
                h_s[:, c * LANES:(c + 1) * LANES] = xs_ref[pl.ds(c, tile, stride=nchunk), :].astype(BF16)

        h = h_s[...]
        a = jnp.dot(h, w1_ref[0].astype(BF16), preferred_element_type=F32)
        b = jnp.dot(h, w3_ref[0].astype(BF16), preferred_element_type=F32)
        part = jnp.dot((_silu(a) * b).astype(BF16), w2_ref[0].astype(BF16), preferred_element_type=F32)

        @pl.when(j == 0)
        def _():
            acc_s[...] = part

        @pl.when(j > 0)
        def _():
            acc_s[...] += part

        @pl.when(j == pl.num_programs(1) - 1)
        def _():
            for c in range(nchunk):
                ys_ref[pl.ds(c, tile, stride=nchunk), :] = acc_s[:, c * LANES:(c + 1) * LANES]


def _moe_experts(xs, counts, w1, w3, w2, cap, tile, tf, n_assign):
    rows, s, _ = xs.shape
    ne, d, dff = w1.shape
    nj = dff // tf
    n_tiles = -(-n_assign // tile) + ne
    tiles_e = (counts + tile - 1) // tile
    cum = jnp.cumsum(tiles_e)
    total = cum[-1]
    t = jnp.arange(n_tiles, dtype=jnp.int32)
    t_eff = jnp.minimum(t, total - 1)
    e_of_t = jnp.minimum(jnp.sum(t_eff[:, None] >= cum[None, :], axis=1), ne - 1).astype(jnp.int32)
    blk = (e_of_t * (cap // tile) + t_eff - (cum - tiles_e)[e_of_t]).astype(jnp.int32)
    valid = (t < total).astype(jnp.int32)
    jmap = lambda j, v: j * v + (nj - 1) * (1 - v)
    grid_spec = pltpu.PrefetchScalarGridSpec(
        num_scalar_prefetch=3,
        grid=(n_tiles, nj),
        in_specs=[pl.BlockSpec((tile * s, LANES), lambda t, j, blk, ex, va: (blk[t], 0)),
                  pl.BlockSpec((1, d, tf), lambda t, j, blk, ex, va: (ex[t], 0, jmap(j, va[t]))),
                  pl.BlockSpec((1, d, tf), lambda t, j, blk, ex, va: (ex[t], 0, jmap(j, va[t]))),
                  pl.BlockSpec((1, tf, d), lambda t, j, blk, ex, va: (ex[t], jmap(j, va[t]), 0))],
        out_specs=pl.BlockSpec((tile * s, LANES), lambda t, j, blk, ex, va: (blk[t], 0)),
        scratch_shapes=[pltpu.VMEM((tile, d), BF16), pltpu.VMEM((tile, d), F32)])
    ys = pl.pallas_call(
        _experts_kernel,
        grid_spec=grid_spec,
        out_shape=jax.ShapeDtypeStruct((rows * s, LANES), F32),
        compiler_params=_params("arbitrary", "arbitrary"),
        name="moe_experts",
    )(blk, e_of_t, valid, xs.reshape(rows * s, LANES), w1, w3, w2)
    return ys.reshape(rows, s, LANES)


def _combine_kernel(dest_ref, x1_ref, info_ref, ys_ref, y_ref, buf, sem):
    tm, d = x1_ref.shape
    nchunk = d // LANES

    def slab(k):
        return buf.at[pl.ds(pl.multiple_of(k * nchunk, nchunk), nchunk)]

    def copies(r):
        return (pltpu.make_async_copy(ys_ref.at[dest_ref[0, 0, r]], slab(r), sem),
                pltpu.make_async_copy(ys_ref.at[dest_ref[0, 0, tm + r]], slab(tm + r), sem))

    def issue(r, carry):
        for cp in copies(r):
            cp.start()
        return carry

    def drain(r, carry):
        for cp in copies(r):
            cp.wait()
        return carry

    lax.fori_loop(0, tm, issue, 0)
    lax.fori_loop(0, tm, drain, 0)
    g1 = info_ref[:, 2:3]
    g2 = info_ref[:, 3:4]
    for c in range(nchunk):
        cs = slice(c * LANES, (c + 1) * LANES)
        r1 = buf[pl.ds(c, tm, stride=nchunk), :]
        r2 = buf[pl.ds(tm * nchunk + c, tm, stride=nchunk), :]
        y_ref[:, cs] = x1_ref[:, cs] + g1 * r1 + g2 * r2


def _moe_combine(x1, info, dest, ys, tm):
    n, d = x1.shape
    s = d // LANES
    row = lambda i: (i, 0)
    return pl.pallas_call(
        _combine_kernel,
        grid=(n // tm,),
        in_specs=[pl.BlockSpec((1, 1, 2 * tm), lambda i: (i, 0, 0), memory_space=pltpu.SMEM),
                  pl.BlockSpec((tm, d), row), pl.BlockSpec((tm, LANES), row),
                  pl.BlockSpec(memory_space=pl.ANY)],
        out_specs=pl.BlockSpec((tm, d), row),
        out_shape=jax.ShapeDtypeStruct((n, d), F32),
        scratch_shapes=[pltpu.VMEM((2 * tm * s, LANES), F32), pltpu.SemaphoreType.DMA(())],
        compiler_params=_params("arbitrary"),
        name="moe_combine",
    )(dest, x1, info, ys)


def _moe_block(streams, moe_w, w_o):
    moe_norm, router, w1, w3, w2 = moe_w
    n_total = sum(x.shape[0] for x, _ in streams)
    cap = -(-n_total // MOE_TILE) * MOE_TILE
    rows = N_EXPERTS * cap
    counts = jnp.zeros((1, LANES), F32)
    routed = []
    for x, o in streams:
        tm = min(512, x.shape[0])
        x1, h3, info, counts = _moe_router(x, o, w_o, moe_norm, router, counts, tm, cap)
        routed.append((x1, h3, info, _row_dests(info, tm), tm))
    cnt = counts[0, :N_EXPERTS].astype(jnp.int32)
    pad = jnp.stack([jnp.arange(N_EXPERTS, dtype=jnp.int32) * cap + cnt, (-cnt) % MOE_TILE])
    xs = None
    for k, (x1, h3, info, dest, tm) in enumerate(routed):
        if k + 1 < len(routed):
            assert xs is None, "only the first and the final dispatch are distinguished"
            xs = _moe_dispatch(h3, dest, tm, rows)
        else:
            xs = _moe_dispatch(h3, dest, tm, rows, xs=xs, pad=pad, tile=MOE_TILE)
    ys = _moe_experts(xs, cnt, w1, w3, w2, cap, MOE_TILE, MOE_TF, 2 * n_total)
    return [_moe_combine(x1, info, dest, ys, tm) for x1, h3, info, dest, tm in routed]


def _fox_layer(x, b, t, cache, w, ffn_w):
    n, d = x.shape
    tm = min(512, n)
    q, k, v, logf = _fox_proj(x, *w[:6], tm=tm)
    if cache is None:
        c = _cumsum_time(logf.reshape(b, t, FOX_HEADS))
        o = _fox_attn_prompt(q, k, v, c, b, t, tq=min(512, t), tk=min(512, t))
    else:
        ck, cv, cl = cache
        past = ck.shape[1]
        c = _cumsum_time(jnp.concatenate([cl, logf.reshape(b, t, FOX_HEADS)], axis=1))
        o = _fox_attn_sample(q, k, v, ck.reshape(b, past, d), cv.reshape(b, past, d), c, b, t)
    y = _attn_out_ffn(x, o, w[6], *ffn_w, tm=tm, tf=512)
    return y, k, v, logf


def _gdn_mixer(x, b, t, chunk, hist, s0, w):
    norm_g, w_qkv, conv_w, w_beta, w_a, a_log, dt_bias, w_gate, out_gain, _ = w
    tm = min(512, t)
    q, k, v, gate, aux, new_hist = _gdn_proj(x, hist, b, t, tm, norm_g, w_qkv, conv_w, w_beta, w_a,
                                             a_log, dt_bias, w_gate)
    groups_per_seq = GDN_HEADS // (LANES // chunk)
    nb = max(1, min(b, GDN_CHAINS_PER_STEP // groups_per_seq))
    o, s_new = _gdn_chunks(q, k, v, gate, aux, s0, out_gain, b, t, chunk, nb)
    return o, s_new, new_hist


def kernel(x_prompt, x_sample, cache_fox_k, cache_fox_v, cache_fox_logf, state_gdn_s, state_gdn_conv, fox_norm, fox_w_qkv, fox_q_gain, fox_k_gain, fox_w_f, fox_b_f, fox_w_o, ffn_norm, ffn_w1, ffn_w3, ffn_w2, gdn_norm, gdn_w_qkv, gdn_conv_w, gdn_w_beta, gdn_w_a, gdn_a_log, gdn_dt_bias, gdn_w_gate, gdn_out_gain, gdn_w_o, moe_norm, moe_router, moe_w1, moe_w3, moe_w2):
    bp, tp, d = x_prompt.shape
    bs, ts, _ = x_sample.shape
    depth = fox_norm.shape[0] + gdn_norm.shape[0]
    cdim = gdn_w_qkv.shape[2]
    yp = x_prompt.reshape(bp * tp, d)
    ys = x_sample.reshape(bs * ts, d)
    kp_l, vp_l, lp_l, ks_l, vs_l, ls_l = [], [], [], [], [], []
    sp_l, cp_l, ss_l, cs_l = [], [], [], []
    for i in range(depth):
        j = i // 2
        if i % 2 == 0:
            fw = (fox_norm[j], fox_w_qkv[j], fox_q_gain[j], fox_k_gain[j], fox_w_f[j], fox_b_f[j], fox_w_o[j])
            ffw = (ffn_norm[j], ffn_w1[j], ffn_w3[j], ffn_w2[j])
            yp, kp, vp, lp = _fox_layer(yp, bp, tp, None, fw, ffw)
            ys, ks, vs, ls = _fox_layer(ys, bs, ts, (cache_fox_k[j], cache_fox_v[j], cache_fox_logf[j]), fw, ffw)
            kp_l.append(kp.reshape(bp, tp, FOX_HEADS, FOX_DH)); vp_l.append(vp.reshape(bp, tp, FOX_HEADS, FOX_DH))
            lp_l.append(lp.reshape(bp, tp, FOX_HEADS))
            ks_l.append(ks.reshape(bs, ts, FOX_HEADS, FOX_DH)); vs_l.append(vs.reshape(bs, ts, FOX_HEADS, FOX_DH))
            ls_l.append(ls.reshape(bs, ts, FOX_HEADS))
        else:
            gw = (gdn_norm[j], gdn_w_qkv[j], gdn_conv_w[j], gdn_w_beta[j], gdn_w_a[j], gdn_a_log[j],
                  gdn_dt_bias[j], gdn_w_gate[j], gdn_out_gain[j], gdn_w_o[j])
            mw = (moe_norm[j], moe_router[j], moe_w1[j], moe_w3[j], moe_w2[j])
            hist0 = jnp.zeros((bp, GDN_CONV - 1, cdim), F32)
            s_zero = jnp.zeros((bp, GDN_HEADS, GDN_DH, GDN_DH), F32)
            op, sp, cp = _gdn_mixer(yp, bp, tp, 64, hist0, s_zero, gw)
            os_, ss, cs = _gdn_mixer(ys, bs, ts, ts, state_gdn_conv[j], state_gdn_s[j], gw)
            yp, ys = _moe_block([(yp, op), (ys, os_)], mw, gw[-1])
            sp_l.append(sp); cp_l.append(cp); ss_l.append(ss); cs_l.append(cs)
    return (yp.reshape(bp, tp, d), ys.reshape(bs, ts, d),
            jnp.stack(kp_l), jnp.stack(vp_l), jnp.stack(lp_l),
            jnp.stack(ks_l), jnp.stack(vs_l), jnp.stack(ls_l),
            jnp.stack(sp_l), jnp.stack(cp_l), jnp.stack(ss_l), jnp.stack(cs_l))
```

```python
import functools

import jax
import jax.numpy as jnp
from jax import lax
from jax.experimental import pallas as pl
from jax.experimental.pallas import tpu as pltpu

F32 = jnp.float32
BF16 = jnp.bfloat16
EPS = 1e-6
LANES = 128
VMEM_LIMIT = 56 * 1024 * 1024

FOX_HEADS = 16
FOX_DH = 64
GDN_HEADS = 8
GDN_DH = 128
GDN_CONV = 4
N_EXPERTS = 8
MOE_TILE = 1024
MOE_TF = 512
GDN_CHAINS_PER_STEP = 16
HIGHEST = lax.Precision.HIGHEST


def _params(*sem):
    return pltpu.CompilerParams(dimension_semantics=sem, vmem_limit_bytes=VMEM_LIMIT)


def _rms(x, gain):
    return x * lax.rsqrt(jnp.mean(x * x, axis=-1, keepdims=True) + EPS) * gain


def _silu(x):
    return x * (1.0 / (1.0 + jnp.exp(-x)))


def _softplus(x):
    return jnp.maximum(x, 0.0) + jnp.log(1.0 + jnp.exp(-jnp.abs(x)))


def _split3(c):
    hi = c.astype(BF16).astype(F32)
    r = c - hi
    mid = r.astype(BF16).astype(F32)
    lo = (r - mid).astype(BF16).astype(F32)
    return hi, mid, lo


def _fox_proj_kernel(x_ref, g_ref, w_ref, qg_ref, kg_ref, bf_ref, gm_ref, gmt_ref,
                     q_ref, k_ref, v_ref, lf_ref):
    d = x_ref.shape[1]
    h = _rms(x_ref[...], g_ref[...])
    y = jnp.dot(h.astype(BF16), w_ref[...], preferred_element_type=F32)

    def headnorm(z, gain):
        ss = jnp.dot((z * z).astype(BF16), gm_ref[...], preferred_element_type=F32)
        inv = lax.rsqrt(ss * (1.0 / FOX_DH) + EPS)
        hi = inv.astype(BF16)
        lo = (inv - hi.astype(F32)).astype(BF16)
        full = (jnp.dot(hi, gmt_ref[...], preferred_element_type=F32)
                + jnp.dot(lo, gmt_ref[...], preferred_element_type=F32))
        return z * full * gain

    q_ref[...] = (headnorm(y[:, :d], qg_ref[...]) * (FOX_DH ** -0.5)).astype(q_ref.dtype)
    k_ref[...] = headnorm(y[:, d:2 * d], kg_ref[...])
    v_ref[...] = y[:, 2 * d:3 * d]
    z = y[:, 3 * d:3 * d + LANES] + bf_ref[...]
    lf = jnp.minimum(z, 0.0) - jnp.log(1.0 + jnp.exp(-jnp.abs(z)))
    lf_ref[...] = lf[:, :FOX_HEADS]


def _fox_proj(x, norm_g, w_qkv, q_gain, k_gain, w_f, b_f, tm):
    n, d = x.shape
    w_all = jnp.concatenate(
        [w_qkv, w_f, jnp.zeros((d, LANES - FOX_HEADS), F32)], axis=1).astype(BF16)
    bf = jnp.concatenate([b_f, jnp.zeros((LANES - FOX_HEADS,), F32)])[None, :]
    head_of_col = jnp.arange(d) // FOX_DH
    gm = (head_of_col[:, None] == jnp.arange(LANES)[None, :]).astype(BF16)
    gmt = gm.T
    qg = jnp.tile(q_gain, FOX_HEADS)[None, :]
    kg = jnp.tile(k_gain, FOX_HEADS)[None, :]
    const = lambda i: (0, 0)
    row = lambda i: (i, 0)
    return pl.pallas_call(
        _fox_proj_kernel,
        grid=(n // tm,),
        in_specs=[pl.BlockSpec((tm, d), row), pl.BlockSpec((1, d), const),
                  pl.BlockSpec((d, 3 * d + LANES), const), pl.BlockSpec((1, d), const),
                  pl.BlockSpec((1, d), const), pl.BlockSpec((1, LANES), const),
                  pl.BlockSpec((d, LANES), const), pl.BlockSpec((LANES, d), const)],
        out_specs=[pl.BlockSpec((tm, d), row), pl.BlockSpec((tm, d), row), pl.BlockSpec((tm, d), row),
                   pl.BlockSpec((tm, FOX_HEADS), row)],
        out_shape=[jax.ShapeDtypeStruct((n, d), BF16), jax.ShapeDtypeStruct((n, d), F32),
                   jax.ShapeDtypeStruct((n, d), F32), jax.ShapeDtypeStruct((n, FOX_HEADS), F32)],
        compiler_params=_params("parallel"),
        name="fox_proj",
    )(x, norm_g[None, :], w_all, qg, kg, bf, gm, gmt)


def _cumsum_kernel(x_ref, o_ref):
    x = x_ref[0]
    t = x.shape[0]
    row = lax.broadcasted_iota(jnp.int32, x.shape, 0)
    s = 1
    while s < t:
        x = x + jnp.where(row >= s, pltpu.roll(x, s, axis=0), 0.0)
        s *= 2
    o_ref[0] = x


def _cumsum_time(x):
    b, t, hn = x.shape
    spec = pl.BlockSpec((1, t, hn), lambda i: (i, 0, 0))
    return pl.pallas_call(
        _cumsum_kernel, grid=(b,), in_specs=[spec], out_specs=spec,
        out_shape=jax.ShapeDtypeStruct(x.shape, F32),
        compiler_params=_params("parallel"), name="logf_cumsum",
    )(x)


def _head_col(c16, head):
    lane = lax.broadcasted_iota(jnp.int32, c16.shape, 1)
    return jnp.sum(jnp.where(lane == head, c16, 0.0), axis=1, keepdims=True)


def _aug(data, ccol, hh, is_q):
    lane = lax.broadcasted_iota(jnp.int32, data.shape, 1)
    base = FOX_DH if hh == 0 else 0
    in_data = (lane < FOX_DH) if hh == 0 else (lane >= FOX_DH)
    hi, mid, lo = _split3(ccol)
    lane1 = lax.broadcasted_iota(jnp.int32, (1, data.shape[1]), 1)
    if is_q:
        pieces = {0: hi, 1: mid, 2: lo}
        ones = (lane1 >= base + 3) & (lane1 < base + 6)
    else:
        pieces = {3: -hi, 4: -mid, 5: -lo}
        ones = (lane1 >= base) & (lane1 < base + 3)
    aug = jnp.where(ones, 1.0, 0.0)
    for i, p in pieces.items():
        aug = jnp.where(lane == base + i, p, aug)
    return jnp.where(in_data, data, aug)


def _v_aug(v2, hh):
    lane = lax.broadcasted_iota(jnp.int32, v2.shape, 1)
    in_data = (lane < FOX_DH) if hh == 0 else (lane >= FOX_DH)
    return jnp.where(in_data, v2, 1.0)


def _finish_heads(acc0, acc1):
    lane = lax.broadcasted_iota(jnp.int32, acc0.shape, 1)
    o0 = acc0 / pltpu.roll(acc0, FOX_DH, axis=1)
    o1 = acc1 / pltpu.roll(acc1, FOX_DH, axis=1)
    return jnp.where(lane < FOX_DH, o0, o1)


def _fox_attn_kernel(q_ref, k_ref, v_ref, c_ref, o_ref, kt_s, v_s, *, tq, tk):
    hp = pl.program_id(1)
    t = k_ref.shape[0]
    nkb = t // tk
    nd = tq // tk
    c16 = c_ref[0]
    k2 = k_ref[...]
    v2 = v_ref[...]
    ccols = [_head_col(c16, 2 * hp + hh) for hh in range(2)]
    for hh in range(2):
        kaug = _aug(k2, ccols[hh], hh, False)
        for jb in range(nkb):
            kt_s[hh, jb] = kaug[jb * tk:(jb + 1) * tk, :].T.astype(BF16)
        v_s[hh] = _v_aug(v2, hh).astype(BF16)

    rows = lax.broadcasted_iota(jnp.int32, (tq, tk), 0)
    cols = lax.broadcasted_iota(jnp.int32, (tq, tk), 1)
    chains = []
    for i in range(t // tq):
        q2 = q_ref[i * tq:(i + 1) * tq, :].astype(F32)
        for hh in range(2):
            qb = _aug(q2, ccols[hh][i * tq:(i + 1) * tq], hh, True).astype(BF16)
            order = list(range(i * nd, (i + 1) * nd)) + list(range(i * nd))
            chains.append(dict(i=i, hh=hh, qb=qb, order=order, m=None, acc=None))
    for rnd in range(max(len(c["order"]) for c in chains)):
        live = [c for c in chains if rnd < len(c["order"])]
        for c in live:
            j = c["order"][rnd]
            s = jnp.dot(c["qb"], kt_s[c["hh"], j], preferred_element_type=F32)
            dblk = j - c["i"] * nd
            if dblk >= 0:
                s = jnp.where(cols + dblk * tk <= rows, s, -jnp.inf)
            c["s"] = s
        for c in live:
            bm = jnp.max(c["s"], axis=1, keepdims=True)
            m_new = bm if c["m"] is None else jnp.maximum(c["m"], bm)
            c["p"] = jnp.exp(c["s"] - m_new).astype(BF16)
            c["alpha"] = None if c["m"] is None else jnp.exp(c["m"] - m_new)
            c["m"] = m_new
        for c in live:
            j = c["order"][rnd]
            pv = jnp.dot(c["p"], v_s[c["hh"], j * tk:(j + 1) * tk, :], preferred_element_type=F32)
            c["acc"] = pv if c["acc"] is None else c["alpha"] * c["acc"] + pv
    for i in range(t // tq):
        o_ref[i * tq:(i + 1) * tq, :] = _finish_heads(chains[2 * i]["acc"],
                                                       chains[2 * i + 1]["acc"]).astype(o_ref.dtype)


def _fox_attn_prompt(q, k, v, c, b, t, tq, tk):
    n, d = q.shape
    nhp = d // LANES
    blk = pl.BlockSpec((t, LANES), lambda bi, hp: (bi, hp))
    return pl.pallas_call(
        functools.partial(_fox_attn_kernel, tq=tq, tk=tk),
        grid=(b, nhp),
        in_specs=[blk, blk, blk, pl.BlockSpec((1, t, FOX_HEADS), lambda bi, hp: (bi, 0, 0))],
        out_specs=blk,
        out_shape=jax.ShapeDtypeStruct((n, d), BF16),
        scratch_shapes=[pltpu.VMEM((2, t // tk, LANES, tk), BF16), pltpu.VMEM((2, t, LANES), BF16)],
        compiler_params=_params("parallel", "parallel"),
        name="fox_attn_prompt",
    )(q, k, v, c)


def _fox_attn_sample_kernel(q_ref, kn_ref, vn_ref, kc_ref, vc_ref, c_ref, o_ref):
    hp = pl.program_id(1)
    t = q_ref.shape[0]
    past = kc_ref.shape[0]
    c16 = c_ref[0]
    q2 = q_ref[...].astype(F32)
    rows = lax.broadcasted_iota(jnp.int32, (t, t), 0)
    cols = lax.broadcasted_iota(jnp.int32, (t, t), 1)
    nt = (((1,), (1,)), ((), ()))
    accs = []
    for hh in range(2):
        ccol = _head_col(c16, 2 * hp + hh)
        qaug = _aug(q2, ccol[past:], hh, True).astype(BF16)
        kc = _aug(kc_ref[...], ccol[:past], hh, False).astype(BF16)
        kn = _aug(kn_ref[...], ccol[past:], hh, False).astype(BF16)
        s_c = lax.dot_general(qaug, kc, nt, preferred_element_type=F32)
        s_n = lax.dot_general(qaug, kn, nt, preferred_element_type=F32)
        s_n = jnp.where(cols <= rows, s_n, -jnp.inf)
        m = jnp.maximum(jnp.max(s_c, axis=1, keepdims=True), jnp.max(s_n, axis=1, keepdims=True))
        p_c = jnp.exp(s_c - m).astype(BF16)
        p_n = jnp.exp(s_n - m).astype(BF16)
        acc = (jnp.dot(p_c, _v_aug(vc_ref[...], hh).astype(BF16), preferred_element_type=F32)
               + jnp.dot(p_n, _v_aug(vn_ref[...], hh).astype(BF16), preferred_element_type=F32))
        accs.append(acc)
    o_ref[...] = _finish_heads(accs[0], accs[1]).astype(o_ref.dtype)


def _fox_attn_sample(q, k_new, v_new, cache_k, cache_v, c_all, b, t):
    n, d = q.shape
    past = cache_k.shape[1]
    nhp = d // LANES
    new_spec = pl.BlockSpec((t, LANES), lambda bi, hp: (bi, hp))
    cache_spec = pl.BlockSpec((None, past, LANES), lambda bi, hp: (bi, 0, hp))
    return pl.pallas_call(
        _fox_attn_sample_kernel,
        grid=(b, nhp),
        in_specs=[new_spec, new_spec, new_spec, cache_spec, cache_spec,
                  pl.BlockSpec((1, past + t, FOX_HEADS), lambda bi, hp: (bi, 0, 0))],
        out_specs=new_spec,
        out_shape=jax.ShapeDtypeStruct((n, d), BF16),
        compiler_params=_params("parallel", "parallel"),
        name="fox_attn_sample",
    )(q, k_new, v_new, cache_k, cache_v, c_all)


def _swiglu_step(h, w1, w3, w2, nsplit=2):
    tf = w1.shape[1]
    g = tf // nsplit
    ups = []
    for s in range(nsplit):
        cs = slice(s * g, (s + 1) * g)
        ups.append((jnp.dot(h, w1[:, cs].astype(BF16), preferred_element_type=F32),
                    jnp.dot(h, w3[:, cs].astype(BF16), preferred_element_type=F32)))
    out = None
    for s, (a, b) in enumerate(ups):
        t = (_silu(a) * b).astype(BF16)
        part = jnp.dot(t, w2[s * g:(s + 1) * g, :].astype(BF16), preferred_element_type=F32)
        out = part if out is None else out + part
    return out


def _ffn_kernel(x_ref, o_ref, wo_ref, g_ref, w1_ref, w3_ref, w2_ref, y_ref, h_s):
    j = pl.program_id(1)

    @pl.when(j == 0)
    def _():
        x1 = x_ref[...] + jnp.dot(o_ref[...], wo_ref[...], preferred_element_type=F32)
        y_ref[...] = x1
        h_s[...] = _rms(x1, g_ref[...]).astype(BF16)

    y_ref[...] += _swiglu_step(h_s[...], w1_ref[...], w3_ref[...], w2_ref[...])


def _attn_out_ffn(x, o, w_o, norm_g, w1, w3, w2, tm, tf):
    n, d = x.shape
    dff = w1.shape[1]
    row = lambda i, j: (i, 0)
    const = lambda i, j: (0, 0)
    return pl.pallas_call(
        _ffn_kernel,
        grid=(n // tm, dff // tf),
        in_specs=[pl.BlockSpec((tm, d), row), pl.BlockSpec((tm, d), row), pl.BlockSpec((d, d), const),
                  pl.BlockSpec((1, d), const), pl.BlockSpec((d, tf), lambda i, j: (0, j)),
                  pl.BlockSpec((d, tf), lambda i, j: (0, j)), pl.BlockSpec((tf, d), lambda i, j: (j, 0))],
        out_specs=pl.BlockSpec((tm, d), row),
        out_shape=jax.ShapeDtypeStruct((n, d), F32),
        scratch_shapes=[pltpu.VMEM((tm, d), BF16)],
        compiler_params=_params("parallel", "arbitrary"),
        name="attn_out_ffn",
    )(x, o, w_o.astype(BF16), norm_g[None, :], w1.astype(BF16), w3.astype(BF16), w2.astype(BF16))


def _gdn_proj_kernel(x_ref, g_ref, w_ref, cw_ref, hist_ref, alog_ref, dtb_ref,
                     q_ref, k_ref, v_ref, gate_ref, aux_ref, hout_ref, pad_s, *, steps_per_seq):
    i = pl.program_id(0)
    tm, d = x_ref.shape
    cdim = 3 * d
    first = (i % steps_per_seq) == 0

    @pl.when(first)
    def _():
        pad_s[0:8, :] = hist_ref[0]

    @pl.when(jnp.logical_not(first))
    def _():
        pad_s[0:8, :] = pad_s[tm:tm + 8, :]

    h = _rms(x_ref[...], g_ref[...])
    y = jnp.dot(h.astype(BF16), w_ref[...], preferred_element_type=F32)
    pad_s[8:8 + tm, :] = y[:, :cdim]
    hout_ref[0] = pad_s[tm:tm + 8, :]

    for blk in range(cdim // LANES):
        cs = slice(blk * LANES, (blk + 1) * LANES)
        mixed = jnp.zeros((tm, LANES), F32)
        for w in range(GDN_CONV):
            mixed = mixed + cw_ref[w:w + 1, cs] * pad_s[8 - (GDN_CONV - 1) + w:8 - (GDN_CONV - 1) + w + tm, cs]
        mixed = _silu(mixed)
        part, hd = divmod(blk, GDN_HEADS)
        hs = slice(hd * LANES, (hd + 1) * LANES)
        if part == 2:
            v_ref[:, hs] = mixed
        else:
            nrm = mixed * lax.rsqrt(jnp.sum(mixed * mixed, axis=1, keepdims=True) + EPS)
            if part == 0:
                q_ref[:, hs] = nrm * (GDN_DH ** -0.5)
            else:
                k_ref[:, hs] = nrm

    gate_ref[...] = _silu(y[:, cdim:cdim + d])
    z = y[:, cdim + d:cdim + d + LANES]
    lane = lax.broadcasted_iota(jnp.int32, z.shape, 1)
    beta = 1.0 / (1.0 + jnp.exp(-z))
    gdec = -jnp.exp(alog_ref[...]) * _softplus(z + dtb_ref[...])
    aux_ref[...] = jnp.where(lane < GDN_HEADS, beta, gdec)


def _gdn_proj(x, hist, b, t, tm, norm_g, w_qkv, conv_w, w_beta, w_a, a_log, dt_bias, w_gate):
    n, d = x.shape
    cdim = 3 * d
    w_all = jnp.concatenate(
        [w_qkv, w_gate, w_beta, w_a, jnp.zeros((d, LANES - 2 * GDN_HEADS), F32)], axis=1).astype(BF16)
    hist8 = jnp.concatenate([jnp.zeros((b, 8 - (GDN_CONV - 1), cdim), F32), hist], axis=1)
    lane_vec = lambda a: jnp.concatenate(
        [jnp.zeros((GDN_HEADS,), F32), a, jnp.zeros((LANES - 2 * GDN_HEADS,), F32)])[None, :]
    steps = t // tm
    row = lambda i: (i, 0)
    const = lambda i: (0, 0)
    outs = pl.pallas_call(
        functools.partial(_gdn_proj_kernel, steps_per_seq=steps),
        grid=(n // tm,),
        in_specs=[pl.BlockSpec((tm, d), row), pl.BlockSpec((1, d), const),
                  pl.BlockSpec((d, 4 * d + LANES), const), pl.BlockSpec((GDN_CONV, cdim), const),
                  pl.BlockSpec((1, 8, cdim), lambda i: (i // steps, 0, 0)),
                  pl.BlockSpec((1, LANES), const), pl.BlockSpec((1, LANES), const)],
        out_specs=[pl.BlockSpec((tm, d), row), pl.BlockSpec((tm, d), row), pl.BlockSpec((tm, d), row),
                   pl.BlockSpec((tm, d), row), pl.BlockSpec((tm, LANES), row),
                   pl.BlockSpec((1, 8, cdim), lambda i: (i // steps, 0, 0))],
        out_shape=[jax.ShapeDtypeStruct((n, d), F32)] * 4 + [jax.ShapeDtypeStruct((n, LANES), F32),
                                                           jax.ShapeDtypeStruct((b, 8, cdim), F32)],
        scratch_shapes=[pltpu.VMEM((tm + 8, cdim), F32)],
        compiler_params=_params("arbitrary"),
        name="gdn_proj",
    )(x, norm_g[None, :], w_all, conv_w, hist8, lane_vec(a_log), lane_vec(dt_bias))
    q, k, v, gate, aux, hout = outs
    return q, k, v, gate, aux, hout[:, 8 - (GDN_CONV - 1):, :]


def _gdn_chunk_kernel(q_ref, k_ref, v_ref, gate_ref, aux_ref, s0_ref, og_ref, o_ref, s_ref, *, chunk):
    ci = pl.program_id(1)
    nb = q_ref.shape[0]
    per = LANES // chunk
    ngroups = GDN_HEADS // per
    nsq = chunk.bit_length() - 2

    @pl.when(ci == 0)
    def _():
        s_ref[...] = s0_ref[...]

    bi = lax.broadcasted_iota(jnp.int32, (LANES, LANES), 0)
    bj = lax.broadcasted_iota(jnp.int32, (LANES, LANES), 1)
    same = (bi // chunk) == (bj // chunk)
    causal = same & (bi >= bj)
    strict = same & (bi > bj)
    eye = (bi == bj).astype(F32)
    tcol = bj // chunk
    nt = (((1,), (1,)), ((), ()))

    chains = []
    for bb in range(nb):
        aux = aux_ref[bb]
        trow = lax.broadcasted_iota(jnp.int32, aux.shape, 0)
        gc = aux
        sh = 1
        while sh < chunk:
            gc = gc + jnp.where(trow >= sh, pltpu.roll(gc, sh, axis=0), 0.0)
            sh *= 2
        glast = gc[chunk - 1:chunk, :]
        for grp in range(ngroups):
            heads = [grp * per + a for a in range(per)]
            stack = lambda ref: jnp.concatenate(
                [ref[bb, :, hd * LANES:(hd + 1) * LANES] for hd in heads], axis=0)
            col = lambda arr, off: jnp.concatenate([arr[:, off + hd:off + hd + 1] for hd in heads], axis=0)
            c = dict(bb=bb, heads=heads, glast=glast)
            c["q2"], c["k2"], c["v2"] = stack(q_ref), stack(k_ref), stack(v_ref)
            c["beta"] = col(aux, 0)
            gcol = col(gc, GDN_HEADS)
            gl = jnp.concatenate([jnp.broadcast_to(glast[:, GDN_HEADS + hd:GDN_HEADS + hd + 1], (chunk, 1))
                                  for hd in heads], axis=0)
            grow = jnp.broadcast_to(gcol, (LANES, LANES)).T
            c["decay"] = jnp.exp(jnp.where(causal, gcol - grow, -jnp.inf))
            c["egc"] = jnp.exp(gcol)
            c["kdec"] = jnp.exp(gl - gcol)
            chains.append(c)

    for c in chains:
        k2b = c["k2"].astype(BF16)
        c["both"] = lax.dot_general(jnp.concatenate([k2b, c["q2"].astype(BF16)], axis=0), k2b, nt,
                                    preferred_element_type=F32)
    for c in chains:
        c["x"] = -jnp.where(strict, c["beta"] * c["both"][:LANES] * c["decay"], 0.0)
        c["attn"] = (c["both"][LANES:] * c["decay"]).astype(BF16)
        xb = c["x"].astype(BF16)
        c["p"] = jnp.dot(xb, xb, preferred_element_type=F32)
        c["tser"] = eye + c["x"]
    for it in range(nsq):
        for c in chains:
            c["pb"] = c["p"].astype(BF16)
            c["tser"] = c["tser"] + jnp.dot(c["tser"].astype(BF16), c["pb"], preferred_element_type=F32)
        if it + 1 < nsq:
            for c in chains:
                c["p"] = jnp.dot(c["pb"], c["pb"], preferred_element_type=F32)
    for c in chains:
        rhs = jnp.concatenate([c["v2"] * c["beta"], c["k2"] * (c["beta"] * c["egc"])], axis=1)
        c["sol"] = rhs + jnp.dot((c["tser"] - eye).astype(BF16), rhs.astype(BF16), preferred_element_type=F32)
    for c in chains:
        w2b = c["sol"][:, LANES:].astype(BF16)
        qd = (c["q2"] * c["egc"]).astype(BF16)
        c["wq"] = []
        for a, hd in enumerate(c["heads"]):
            rs = slice(a * chunk, (a + 1) * chunk)
            c["wq"].append(jnp.dot(jnp.concatenate([w2b[rs], qd[rs]], axis=0), s_ref[c["bb"], hd].astype(BF16),
                                   preferred_element_type=F32))
    for c in chains:
        ws = jnp.concatenate([wq[:chunk] for wq in c["wq"]], axis=0)
        c["qs"] = jnp.concatenate([wq[chunk:] for wq in c["wq"]], axis=0)
        c["vnb"] = (c["sol"][:, :LANES] - ws).astype(BF16)
        c["kdt"] = (c["k2"] * c["kdec"]).T
    for c in chains:
        c["o2"] = c["qs"] + jnp.dot(c["attn"], c["vnb"], preferred_element_type=F32)
        for a, hd in enumerate(c["heads"]):
            kdt_h = jnp.where(tcol == a, c["kdt"], 0.0).astype(BF16)
            decay_s = jnp.exp(c["glast"][:, GDN_HEADS + hd:GDN_HEADS + hd + 1])
            s_ref[c["bb"], hd] = (s_ref[c["bb"], hd] * decay_s
                                  + jnp.dot(kdt_h, c["vnb"], preferred_element_type=F32))
    for c in chains:
        for a, hd in enumerate(c["heads"]):
            rs = slice(a * chunk, (a + 1) * chunk)
            oh = _rms(c["o2"][rs], og_ref[...]) * gate_ref[c["bb"], :, hd * LANES:(hd + 1) * LANES]
            o_ref[c["bb"], :, hd * LANES:(hd + 1) * LANES] = oh.astype(o_ref.dtype)


def _gdn_chunks(q, k, v, gate, aux, s0, out_gain, b, t, chunk, nb):
    n, d = q.shape
    nc = t // chunk
    seq = lambda a: a.reshape(b, t, a.shape[1])
    row = lambda bi, ci: (bi, ci, 0)
    st = lambda bi, ci: (bi, 0, 0, 0)
    sshape = (nb, GDN_HEADS, GDN_DH, GDN_DH)
    o, s_new = pl.pallas_call(
        functools.partial(_gdn_chunk_kernel, chunk=chunk),
        grid=(b // nb, nc),
        in_specs=[pl.BlockSpec((nb, chunk, d), row)] * 4 + [pl.BlockSpec((nb, chunk, LANES), row),
                                                           pl.BlockSpec(sshape, st),
                                                           pl.BlockSpec((1, GDN_DH), lambda bi, ci: (0, 0))],
        out_specs=[pl.BlockSpec((nb, chunk, d), row), pl.BlockSpec(sshape, st)],
        out_shape=[jax.ShapeDtypeStruct((b, t, d), BF16), jax.ShapeDtypeStruct(s0.shape, F32)],
        compiler_params=_params("parallel", "arbitrary"),
        name="gdn_chunks",
    )(seq(q), seq(k), seq(v), seq(gate), seq(aux), s0, out_gain[None, :])
    return o.reshape(n, d), s_new


def _router_kernel(x_ref, o_ref, wo_ref, g_ref, r_ref, base_ref, x1_ref, h3_ref, info_ref, cnt_ref, cnt_s,
                   *, cap):
    i = pl.program_id(0)
    tm, d = x_ref.shape

    @pl.when(i == 0)
    def _():
        cnt_s[...] = base_ref[...]

    x1 = x_ref[...] + jnp.dot(o_ref[...], wo_ref[...], preferred_element_type=F32)
    x1_ref[...] = x1
    h = _rms(x1, g_ref[...])
    nchunk = d // LANES
    for c in range(nchunk):
        h3_ref[pl.ds(c, tm, stride=nchunk), :] = h[:, c * LANES:(c + 1) * LANES]
    logits = jnp.dot(h, r_ref[...], preferred_element_type=F32, precision=HIGHEST)
    lane = lax.broadcasted_iota(jnp.int32, logits.shape, 1)
    logits = jnp.where(lane < N_EXPERTS, logits, -jnp.inf)
    m1 = jnp.max(logits, axis=1, keepdims=True)
    i1 = jnp.min(jnp.where(logits == m1, lane, LANES), axis=1, keepdims=True)
    rest = jnp.where(lane == i1, -jnp.inf, logits)
    m2 = jnp.max(rest, axis=1, keepdims=True)
    i2 = jnp.min(jnp.where(rest == m2, lane, LANES), axis=1, keepdims=True)
    e2 = jnp.exp(m2 - m1)
    g1 = 1.0 / (1.0 + e2)
    g2 = e2 / (1.0 + e2)
    member = ((lane == i1) | (lane == i2)).astype(F32)
    ri = lax.broadcasted_iota(jnp.int32, (tm, tm), 0)
    rj = lax.broadcasted_iota(jnp.int32, (tm, tm), 1)
    rank = jnp.dot((rj < ri).astype(BF16), member.astype(BF16), preferred_element_type=F32)
    slot = cnt_s[...] + rank + lane.astype(F32) * float(cap)
    d1 = jnp.sum(jnp.where(lane == i1, slot, 0.0), axis=1, keepdims=True)
    d2 = jnp.sum(jnp.where(lane == i2, slot, 0.0), axis=1, keepdims=True)
    cnt_s[...] += jnp.sum(member, axis=0, keepdims=True)
    cnt_ref[...] = cnt_s[...]
    info = jnp.zeros(logits.shape, F32)
    for k, val in enumerate((i1.astype(F32), i2.astype(F32), g1, g2, d1, d2)):
        info = jnp.where(lane == k, val, info)
    info_ref[...] = info


def _moe_router(x, o, w_o, norm_g, router, base_counts, tm, cap):
    n, d = x.shape
    r_pad = jnp.concatenate([router, jnp.zeros((d, LANES - N_EXPERTS), F32)], axis=1)
    row = lambda i: (i, 0)
    const = lambda i: (0, 0)
    x1, h3, info, counts = pl.pallas_call(
        functools.partial(_router_kernel, cap=cap),
        grid=(n // tm,),
        in_specs=[pl.BlockSpec((tm, d), row), pl.BlockSpec((tm, d), row), pl.BlockSpec((d, d), const),
                  pl.BlockSpec((1, d), const), pl.BlockSpec((d, LANES), const), pl.BlockSpec((1, LANES), const)],
        out_specs=[pl.BlockSpec((tm, d), row), pl.BlockSpec((tm * (d // LANES), LANES), row),
                   pl.BlockSpec((tm, LANES), row), pl.BlockSpec((1, LANES), const)],
        out_shape=[jax.ShapeDtypeStruct((n, d), F32), jax.ShapeDtypeStruct((n * (d // LANES), LANES), F32),
                   jax.ShapeDtypeStruct((n, LANES), F32), jax.ShapeDtypeStruct((1, LANES), F32)],
        scratch_shapes=[pltpu.VMEM((1, LANES), F32)],
        compiler_params=_params("arbitrary"),
        name="moe_router",
    )(x, o, w_o.astype(BF16), norm_g[None, :], r_pad, base_counts)
    return x1, h3.reshape(n, d // LANES, LANES), info, counts


def _row_dests(info, tm):
    n = info.shape[0]
    dest = info[:, 4:6].astype(jnp.int32)
    return dest.reshape(n // tm, tm, 2).transpose(0, 2, 1).reshape(n // tm, 1, 2 * tm)


def _dispatch_kernel(*refs, tm, final, tile):
    if final:
        dest_ref, pad_ref, h3_ref, _, xs_ref, sem, zeros_s = refs
    else:
        dest_ref, h3_ref, xs_ref, sem = refs
    i = pl.program_id(0)
    nchunk = xs_ref.shape[1]

    def copies(r):
        src = h3_ref.at[pl.ds(pl.multiple_of(r * nchunk, nchunk), nchunk)]
        return (pltpu.make_async_copy(src, xs_ref.at[dest_ref[0, 0, r]], sem),
                pltpu.make_async_copy(src, xs_ref.at[dest_ref[0, 0, tm + r]], sem))

    def issue(r, carry):
        for cp in copies(r):
            cp.start()
        return carry

    lax.fori_loop(0, tm, issue, 0, unroll=8)
    landed = xs_ref.at[pl.ds(0, 2 * tm)]
    pltpu.make_async_copy(landed, landed, sem).wait()

    if final:
        @pl.when(i == pl.num_programs(0) - 1)
        def _():
            zeros_s[...] = jnp.zeros(zeros_s.shape, F32)
            for e in range(N_EXPERTS):
                start = pad_ref[0, e]
                tail = pad_ref[1, e]
                for bit in reversed(range(tile.bit_length() - 1)):
                    size = 1 << bit
                    off = start + ((tail >> (bit + 1)) << (bit + 1))

                    @pl.when(((tail >> bit) & 1) == 1)
                    def _():
                        cp = pltpu.make_async_copy(zeros_s.at[pl.ds(0, size)], xs_ref.at[pl.ds(off, size)], sem)
                        cp.start()
                        cp.wait()


def _moe_dispatch(h3, dest, tm, rows, xs=None, pad=None, tile=None):
    n, s, _ = h3.shape
    final = xs is not None
    smem = functools.partial(pl.BlockSpec, memory_space=pltpu.SMEM)
    anyspec = pl.BlockSpec(memory_space=pl.ANY)
    in_specs = [smem((1, 1, 2 * tm), lambda i: (i, 0, 0))]
    args = [dest]
    if final:
        in_specs.append(smem((2, N_EXPERTS), lambda i: (0, 0)))
        args.append(pad)
    in_specs.append(pl.BlockSpec((tm * s, LANES), lambda i: (i, 0)))
    args.append(h3.reshape(n * s, LANES))
    scratch = [pltpu.SemaphoreType.DMA(())]
    if final:
        in_specs.append(anyspec)
        args.append(xs)
        scratch.append(pltpu.VMEM((tile // 2, s, LANES), F32))
    return pl.pallas_call(
        functools.partial(_dispatch_kernel, tm=tm, final=final, tile=tile),
        grid=(n // tm,),
        in_specs=in_specs,
        out_specs=anyspec,
        out_shape=jax.ShapeDtypeStruct((rows, s, LANES), F32),
        scratch_shapes=scratch,
        input_output_aliases={len(args) - 1: 0} if final else {},
        compiler_params=_params("arbitrary"),
        name="moe_dispatch",
    )(*args)


def _experts_kernel(blk_ref, exp_ref, valid_ref, xs_ref, w1_ref, w3_ref, w2_ref, ys_ref, h_s, acc_s):
    t = pl.program_id(0)
    j = pl.program_id(1)
    tile, d = h_s.shape
    nchunk = d // LANES

    @pl.when(valid_ref[t] == 1)
    def _():
        @pl.when(j == 0)
        def _():
            for c in range(nchunk):
                h_s[:, c * LANES:(c + 1) * LANES] = xs_ref[pl.ds(c, tile, stride=nchunk), :].astype(BF16)

        part = _swiglu_step(h_s[...], w1_ref[0], w3_ref[0], w2_ref[0])

        @pl.when(j == 0)
        def _():
            acc_s[...] = part

        @pl.when(j > 0)
        def _():
            acc_s[...] += part

        @pl.when(j == pl.num_programs(1) - 1)
        def _():
            for c in range(nchunk):
                ys_ref[pl.ds(c, tile, stride=nchunk), :] = acc_s[:, c * LANES:(c + 1) * LANES]


def _moe_experts(xs, counts, w1, w3, w2, cap, tile, tf, n_assign):
    rows, s, _ = xs.shape
    ne, d, dff = w1.shape
    nj = dff // tf
    n_tiles = -(-n_assign // tile) + ne
    tiles_e = (counts + tile - 1) // tile
    cum = jnp.cumsum(tiles_e)
    total = cum[-1]
    t = jnp.arange(n_tiles, dtype=jnp.int32)
    t_eff = jnp.minimum(t, total - 1)
    e_of_t = jnp.minimum(jnp.sum(t_eff[:, None] >= cum[None, :], axis=1), ne - 1).astype(jnp.int32)
    blk = (e_of_t * (cap // tile) + t_eff - (cum - tiles_e)[e_of_t]).astype(jnp.int32)
    valid = (t < total).astype(jnp.int32)
    jmap = lambda j, v: j * v + (nj - 1) * (1 - v)
    grid_spec = pltpu.PrefetchScalarGridSpec(
        num_scalar_prefetch=3,
        grid=(n_tiles, nj),
        in_specs=[pl.BlockSpec((tile * s, LANES), lambda t, j, blk, ex, va: (blk[t], 0)),
                  pl.BlockSpec((1, d, tf), lambda t, j, blk, ex, va: (ex[t], 0, jmap(j, va[t]))),
                  pl.BlockSpec((1, d, tf), lambda t, j, blk, ex, va: (ex[t], 0, jmap(j, va[t]))),
                  pl.BlockSpec((1, tf, d), lambda t, j, blk, ex, va: (ex[t], jmap(j, va[t]), 0))],
        out_specs=pl.BlockSpec((tile * s, LANES), lambda t, j, blk, ex, va: (blk[t], 0)),
        scratch_shapes=[pltpu.VMEM((tile, d), BF16), pltpu.VMEM((tile, d), F32)])
    ys = pl.pallas_call(
        _experts_kernel,
        grid_spec=grid_spec,
        out_shape=jax.ShapeDtypeStruct((rows * s, LANES), F32),
        compiler_params=_params("arbitrary", "arbitrary"),
        name="moe_experts",
    )(blk, e_of_t, valid, xs.reshape(rows * s, LANES), w1, w3, w2)
    return ys.reshape(rows, s, LANES)


def _combine_kernel(dest_ref, x1_ref, info_ref, ys_ref, y_ref, buf, sem):
    tm, d = x1_ref.shape
    nchunk = d // LANES

    def slab(k):
        return buf.at[pl.ds(pl.multiple_of(k * nchunk, nchunk), nchunk)]

    def copies(r):
        return (pltpu.make_async_copy(ys_ref.at[dest_ref[0, 0, r]], slab(r), sem),
                pltpu.make_async_copy(ys_ref.at[dest_ref[0, 0, tm + r]], slab(tm + r), sem))

    def issue(r, carry):
        for cp in copies(r):
            cp.start()
        return carry

    lax.fori_loop(0, tm, issue, 0, unroll=8)
    pltpu.make_async_copy(buf, buf, sem).wait()
    g1 = info_ref[:, 2:3]
    g2 = info_ref[:, 3:4]
    for c in range(nchunk):
        cs = slice(c * LANES, (c + 1) * LANES)
        r1 = buf[pl.ds(c, tm, stride=nchunk), :]
        r2 = buf[pl.ds(tm * nchunk + c, tm, stride=nchunk), :]
        y_ref[:, cs] = x1_ref[:, cs] + g1 * r1 + g2 * r2


def _moe_combine(x1, info, dest, ys, tm):
    n, d = x1.shape
    s = d // LANES
    row = lambda i: (i, 0)
    return pl.pallas_call(
        _combine_kernel,
        grid=(n // tm,),
        in_specs=[pl.BlockSpec((1, 1, 2 * tm), lambda i: (i, 0, 0), memory_space=pltpu.SMEM),
                  pl.BlockSpec((tm, d), row), pl.BlockSpec((tm, LANES), row),
                  pl.BlockSpec(memory_space=pl.ANY)],
        out_specs=pl.BlockSpec((tm, d), row),
        out_shape=jax.ShapeDtypeStruct((n, d), F32),
        scratch_shapes=[pltpu.VMEM((2 * tm * s, LANES), F32), pltpu.SemaphoreType.DMA(())],
        compiler_params=_params("arbitrary"),
        name="moe_combine",
    )(dest, x1, info, ys)


def _moe_block(streams, moe_w, w_o):
    moe_norm, router, w1, w3, w2 = moe_w
    n_total = sum(x.shape[0] for x, _ in streams)
    cap = -(-n_total // MOE_TILE) * MOE_TILE
    rows = N_EXPERTS * cap
    counts = jnp.zeros((1, LANES), F32)
    routed = []
    for x, o in streams:
        tm = min(512, x.shape[0])
        x1, h3, info, counts = _moe_router(x, o, w_o, moe_norm, router, counts, tm, cap)
        routed.append((x1, h3, info, _row_dests(info, tm), tm))
    cnt = counts[0, :N_EXPERTS].astype(jnp.int32)
    pad = jnp.stack([jnp.arange(N_EXPERTS, dtype=jnp.int32) * cap + cnt, (-cnt) % MOE_TILE])
    xs = None
    for k, (x1, h3, info, dest, tm) in enumerate(routed):
        if k + 1 < len(routed):
            assert xs is None, "only the first and the final dispatch are distinguished"
            xs = _moe_dispatch(h3, dest, tm, rows)
        else:
            xs = _moe_dispatch(h3, dest, tm, rows, xs=xs, pad=pad, tile=MOE_TILE)
    ys = _moe_experts(xs, cnt, w1, w3, w2, cap, MOE_TILE, MOE_TF, 2 * n_total)
    return [_moe_combine(x1, info, dest, ys, tm) for x1, h3, info, dest, tm in routed]


def _fox_layer(x, b, t, cache, w, ffn_w):
    n, d = x.shape
    tm = min(512, n)
    q, k, v, logf = _fox_proj(x, *w[:6], tm=tm)
    if cache is None:
        c = _cumsum_time(logf.reshape(b, t, FOX_HEADS))
        o = _fox_attn_prompt(q, k, v, c, b, t, tq=min(512, t), tk=min(512, t))
    else:
        ck, cv, cl = cache
        past = ck.shape[1]
        c = _cumsum_time(jnp.concatenate([cl, logf.reshape(b, t, FOX_HEADS)], axis=1))
        o = _fox_attn_sample(q, k, v, ck.reshape(b, past, d), cv.reshape(b, past, d), c, b, t)
    y = _attn_out_ffn(x, o, w[6], *ffn_w, tm=tm, tf=512)
    return y, k, v, logf


def _gdn_mixer(x, b, t, chunk, hist, s0, w):
    norm_g, w_qkv, conv_w, w_beta, w_a, a_log, dt_bias, w_gate, out_gain, _ = w
    tm = min(512, t)
    q, k, v, gate, aux, new_hist = _gdn_proj(x, hist, b, t, tm, norm_g, w_qkv, conv_w, w_beta, w_a,
                                             a_log, dt_bias, w_gate)
    groups_per_seq = GDN_HEADS // (LANES // chunk)
    nb = max(1, min(b, GDN_CHAINS_PER_STEP // groups_per_seq))
    o, s_new = _gdn_chunks(q, k, v, gate, aux, s0, out_gain, b, t, chunk, nb)
    return o, s_new, new_hist


def kernel(x_prompt, x_sample, cache_fox_k, cache_fox_v, cache_fox_logf, state_gdn_s, state_gdn_conv, fox_norm, fox_w_qkv, fox_q_gain, fox_k_gain, fox_w_f, fox_b_f, fox_w_o, ffn_norm, ffn_w1, ffn_w3, ffn_w2, gdn_norm, gdn_w_qkv, gdn_conv_w, gdn_w_beta, gdn_w_a, gdn_a_log, gdn_dt_bias, gdn_w_gate, gdn_out_gain, gdn_w_o, moe_norm, moe_router, moe_w1, moe_w3, moe_w2):
    bp, tp, d = x_prompt.shape
    bs, ts, _ = x_sample.shape
    depth = fox_norm.shape[0] + gdn_norm.shape[0]
    cdim = gdn_w_qkv.shape[2]
    yp = x_prompt.reshape(bp * tp, d)
    ys = x_sample.reshape(bs * ts, d)
    kp_l, vp_l, lp_l, ks_l, vs_l, ls_l = [], [], [], [], [], []
    sp_l, cp_l, ss_l, cs_l = [], [], [], []
    for i in range(depth):
        j = i // 2
        if i % 2 == 0:
            fw = (fox_norm[j], fox_w_qkv[j], fox_q_gain[j], fox_k_gain[j], fox_w_f[j], fox_b_f[j], fox_w_o[j])
            ffw = (ffn_norm[j], ffn_w1[j], ffn_w3[j], ffn_w2[j])
            yp, kp, vp, lp = _fox_layer(yp, bp, tp, None, fw, ffw)
            ys, ks, vs, ls = _fox_layer(ys, bs, ts, (cache_fox_k[j], cache_fox_v[j], cache_fox_logf[j]), fw, ffw)
            kp_l.append(kp.reshape(bp, tp, FOX_HEADS, FOX_DH)); vp_l.append(vp.reshape(bp, tp, FOX_HEADS, FOX_DH))
            lp_l.append(lp.reshape(bp, tp, FOX_HEADS))
            ks_l.append(ks.reshape(bs, ts, FOX_HEADS, FOX_DH)); vs_l.append(vs.reshape(bs, ts, FOX_HEADS, FOX_DH))
            ls_l.append(ls.reshape(bs, ts, FOX_HEADS))
        else:
            gw = (gdn_norm[j], gdn_w_qkv[j], gdn_conv_w[j], gdn_w_beta[j], gdn_w_a[j], gdn_a_log[j],
                  gdn_dt_bias[j], gdn_w_gate[j], gdn_out_gain[j], gdn_w_o[j])
            mw = (moe_norm[j], moe_router[j], moe_w1[j], moe_w3[j], moe_w2[j])
            hist0 = jnp.zeros((bp, GDN_CONV - 1, cdim), F32)
            s_zero = jnp.zeros((bp, GDN_HEADS, GDN_DH, GDN_DH), F32)
            op, sp, cp = _gdn_mixer(yp, bp, tp, 64, hist0, s_zero, gw)
            os_, ss, cs = _gdn_mixer(ys, bs, ts, ts, state_gdn_conv[j], state_gdn_s[j], gw)
            yp, ys = _moe_block([(yp, op), (ys, os_)], mw, gw[-1])
            sp_l.append(sp); cp_l.append(cp); ss_l.append(ss); cs_l.append(cs)
    return (yp.reshape(bp, tp, d), ys.reshape(bs, ts, d),
            jnp.stack(kp_l), jnp.stack(vp_l), jnp.stack(lp_l),
            jnp.stack(ks_l), jnp.stack(vs_l), jnp.stack(ls_l),
            jnp.stack(sp_l), jnp.stack(cp_l), jnp.stack(ss_l), jnp.stack(cs_l))
```

```python
import functools

import jax
import jax.numpy as jnp
from jax import lax
from jax.experimental import pallas as pl
from jax.experimental.pallas import tpu as pltpu

F32 = jnp.float32
BF16 = jnp.bfloat16
EPS = 1e-6
LANES = 128
VMEM_LIMIT = 56 * 1024 * 1024

FOX_HEADS = 16
FOX_DH = 64
GDN_HEADS = 8
GDN_DH = 128
GDN_CONV = 4
N_EXPERTS = 8
MOE_TILES_PER_EXPERT = 4
MOE_TF = 512
GDN_CHAINS_PER_STEP = 16
HIGHEST = lax.Precision.HIGHEST


def _params(*sem):
    return pltpu.CompilerParams(dimension_semantics=sem, vmem_limit_bytes=VMEM_LIMIT)


def _rms(x, gain):
    return x * lax.rsqrt(jnp.mean(x * x, axis=-1, keepdims=True) + EPS) * gain


def _silu(x):
    return x * (1.0 / (1.0 + jnp.exp(-x)))


def _softplus(x):
    return jnp.maximum(x, 0.0) + jnp.log(1.0 + jnp.exp(-jnp.abs(x)))


def _split3(c):
    hi = c.astype(BF16).astype(F32)
    r = c - hi
    mid = r.astype(BF16).astype(F32)
    lo = (r - mid).astype(BF16).astype(F32)
    return hi, mid, lo


def _fox_proj_kernel(x_ref, g_ref, w_ref, qg_ref, kg_ref, bf_ref, gm_ref, gmt_ref,
                     q_ref, k_ref, v_ref, lf_ref):
    d = x_ref.shape[1]
    h = _rms(x_ref[...], g_ref[...])
    y = jnp.dot(h.astype(BF16), w_ref[...], preferred_element_type=F32)

    def headnorm(z, gain):
        ss = jnp.dot((z * z).astype(BF16), gm_ref[...], preferred_element_type=F32)
        inv = lax.rsqrt(ss * (1.0 / FOX_DH) + EPS)
        hi = inv.astype(BF16)
        lo = (inv - hi.astype(F32)).astype(BF16)
        full = (jnp.dot(hi, gmt_ref[...], preferred_element_type=F32)
                + jnp.dot(lo, gmt_ref[...], preferred_element_type=F32))
        return z * full * gain

    q_ref[...] = (headnorm(y[:, :d], qg_ref[...]) * (FOX_DH ** -0.5)).astype(q_ref.dtype)
    k_ref[...] = headnorm(y[:, d:2 * d], kg_ref[...])
    v_ref[...] = y[:, 2 * d:3 * d]
    z = y[:, 3 * d:3 * d + LANES] + bf_ref[...]
    lf = jnp.minimum(z, 0.0) - jnp.log(1.0 + jnp.exp(-jnp.abs(z)))
    lf_ref[...] = lf[:, :FOX_HEADS]


def _fox_proj(x, norm_g, w_qkv, q_gain, k_gain, w_f, b_f, tm):
    n, d = x.shape
    w_all = jnp.concatenate(
        [w_qkv, w_f, jnp.zeros((d, LANES - FOX_HEADS), F32)], axis=1).astype(BF16)
    bf = jnp.concatenate([b_f, jnp.zeros((LANES - FOX_HEADS,), F32)])[None, :]
    head_of_col = jnp.arange(d) // FOX_DH
    gm = (head_of_col[:, None] == jnp.arange(LANES)[None, :]).astype(BF16)
    gmt = gm.T
    qg = jnp.tile(q_gain, FOX_HEADS)[None, :]
    kg = jnp.tile(k_gain, FOX_HEADS)[None, :]
    const = lambda i: (0, 0)
    row = lambda i: (i, 0)
    return pl.pallas_call(
        _fox_proj_kernel,
        grid=(n // tm,),
        in_specs=[pl.BlockSpec((tm, d), row), pl.BlockSpec((1, d), const),
                  pl.BlockSpec((d, 3 * d + LANES), const), pl.BlockSpec((1, d), const),
                  pl.BlockSpec((1, d), const), pl.BlockSpec((1, LANES), const),
                  pl.BlockSpec((d, LANES), const), pl.BlockSpec((LANES, d), const)],
        out_specs=[pl.BlockSpec((tm, d), row), pl.BlockSpec((tm, d), row), pl.BlockSpec((tm, d), row),
                   pl.BlockSpec((tm, FOX_HEADS), row)],
        out_shape=[jax.ShapeDtypeStruct((n, d), BF16), jax.ShapeDtypeStruct((n, d), F32),
                   jax.ShapeDtypeStruct((n, d), F32), jax.ShapeDtypeStruct((n, FOX_HEADS), F32)],
        compiler_params=_params("parallel"),
        name="fox_proj",
    )(x, norm_g[None, :], w_all, qg, kg, bf, gm, gmt)


def _cumsum_kernel(x_ref, o_ref):
    x = x_ref[0]
    t = x.shape[0]
    row = lax.broadcasted_iota(jnp.int32, x.shape, 0)
    s = 1
    while s < t:
        x = x + jnp.where(row >= s, pltpu.roll(x, s, axis=0), 0.0)
        s *= 2
    o_ref[0] = x


def _cumsum_time(x):
    b, t, hn = x.shape
    spec = pl.BlockSpec((1, t, hn), lambda i: (i, 0, 0))
    return pl.pallas_call(
        _cumsum_kernel, grid=(b,), in_specs=[spec], out_specs=spec,
        out_shape=jax.ShapeDtypeStruct(x.shape, F32),
        compiler_params=_params("parallel"), name="logf_cumsum",
    )(x)


def _head_col(c16, head):
    lane = lax.broadcasted_iota(jnp.int32, c16.shape, 1)
    return jnp.sum(jnp.where(lane == head, c16, 0.0), axis=1, keepdims=True)


def _aug(data, ccol, hh, is_q):
    lane = lax.broadcasted_iota(jnp.int32, data.shape, 1)
    base = FOX_DH if hh == 0 else 0
    in_data = (lane < FOX_DH) if hh == 0 else (lane >= FOX_DH)
    hi, mid, lo = _split3(ccol)
    lane1 = lax.broadcasted_iota(jnp.int32, (1, data.shape[1]), 1)
    if is_q:
        pieces = {0: hi, 1: mid, 2: lo}
        ones = (lane1 >= base + 3) & (lane1 < base + 6)
    else:
        pieces = {3: -hi, 4: -mid, 5: -lo}
        ones = (lane1 >= base) & (lane1 < base + 3)
    aug = jnp.where(ones, 1.0, 0.0)
    for i, p in pieces.items():
        aug = jnp.where(lane == base + i, p, aug)
    return jnp.where(in_data, data, aug)


def _v_aug(v2, hh):
    lane = lax.broadcasted_iota(jnp.int32, v2.shape, 1)
    in_data = (lane < FOX_DH) if hh == 0 else (lane >= FOX_DH)
    return jnp.where(in_data, v2, 1.0)


def _finish_heads(acc0, acc1):
    lane = lax.broadcasted_iota(jnp.int32, acc0.shape, 1)
    o0 = acc0 / pltpu.roll(acc0, FOX_DH, axis=1)
    o1 = acc1 / pltpu.roll(acc1, FOX_DH, axis=1)
    return jnp.where(lane < FOX_DH, o0, o1)


def _fox_attn_kernel(q_ref, k_ref, v_ref, c_ref, o_ref, kt_s, v_s, *, tq, tk):
    hp = pl.program_id(1)
    t = k_ref.shape[0]
    nkb = t // tk
    nd = tq // tk
    c16 = c_ref[0]
    k2 = k_ref[...]
    v2 = v_ref[...]
    ccols = [_head_col(c16, 2 * hp + hh) for hh in range(2)]
    for hh in range(2):
        kaug = _aug(k2, ccols[hh], hh, False)
        for jb in range(nkb):
            kt_s[hh, jb] = kaug[jb * tk:(jb + 1) * tk, :].T.astype(BF16)
        v_s[hh] = _v_aug(v2, hh).astype(BF16)

    rows = lax.broadcasted_iota(jnp.int32, (tq, tk), 0)
    cols = lax.broadcasted_iota(jnp.int32, (tq, tk), 1)
    chains = []
    for i in range(t // tq):
        q2 = q_ref[i * tq:(i + 1) * tq, :].astype(F32)
        for hh in range(2):
            qb = _aug(q2, ccols[hh][i * tq:(i + 1) * tq], hh, True).astype(BF16)
            order = list(range(i * nd, (i + 1) * nd)) + list(range(i * nd))
            chains.append(dict(i=i, hh=hh, qb=qb, order=order, m=None, acc=None))
    for rnd in range(max(len(c["order"]) for c in chains)):
        live = [c for c in chains if rnd < len(c["order"])]
        for c in live:
            j = c["order"][rnd]
            s = jnp.dot(c["qb"], kt_s[c["hh"], j], preferred_element_type=F32)
            dblk = j - c["i"] * nd
            if dblk >= 0:
                s = jnp.where(cols + dblk * tk <= rows, s, -jnp.inf)
            c["s"] = s
        for c in live:
            bm = jnp.max(c["s"], axis=1, keepdims=True)
            m_new = bm if c["m"] is None else jnp.maximum(c["m"], bm)
            c["p"] = jnp.exp(c["s"] - m_new).astype(BF16)
            c["alpha"] = None if c["m"] is None else jnp.exp(c["m"] - m_new)
            c["m"] = m_new
        for c in live:
            j = c["order"][rnd]
            pv = jnp.dot(c["p"], v_s[c["hh"], j * tk:(j + 1) * tk, :], preferred_element_type=F32)
            c["acc"] = pv if c["acc"] is None else c["alpha"] * c["acc"] + pv
    for i in range(t // tq):
        o_ref[i * tq:(i + 1) * tq, :] = _finish_heads(chains[2 * i]["acc"],
                                                       chains[2 * i + 1]["acc"]).astype(o_ref.dtype)


def _fox_attn_prompt(q, k, v, c, b, t, tq, tk):
    n, d = q.shape
    nhp = d // LANES
    blk = pl.BlockSpec((t, LANES), lambda bi, hp: (bi, hp))
    return pl.pallas_call(
        functools.partial(_fox_attn_kernel, tq=tq, tk=tk),
        grid=(b, nhp),
        in_specs=[blk, blk, blk, pl.BlockSpec((1, t, FOX_HEADS), lambda bi, hp: (bi, 0, 0))],
        out_specs=blk,
        out_shape=jax.ShapeDtypeStruct((n, d), BF16),
        scratch_shapes=[pltpu.VMEM((2, t // tk, LANES, tk), BF16), pltpu.VMEM((2, t, LANES), BF16)],
        compiler_params=_params("parallel", "parallel"),
        name="fox_attn_prompt",
    )(q, k, v, c)


def _fox_attn_sample_kernel(q_ref, kn_ref, vn_ref, kc_ref, vc_ref, c_ref, o_ref):
    hp = pl.program_id(1)
    t = q_ref.shape[0]
    past = kc_ref.shape[0]
    c16 = c_ref[0]
    q2 = q_ref[...].astype(F32)
    rows = lax.broadcasted_iota(jnp.int32, (t, t), 0)
    cols = lax.broadcasted_iota(jnp.int32, (t, t), 1)
    nt = (((1,), (1,)), ((), ()))
    accs = []
    for hh in range(2):
        ccol = _head_col(c16, 2 * hp + hh)
        qaug = _aug(q2, ccol[past:], hh, True).astype(BF16)
        kc = _aug(kc_ref[...], ccol[:past], hh, False).astype(BF16)
        kn = _aug(kn_ref[...], ccol[past:], hh, False).astype(BF16)
        s_c = lax.dot_general(qaug, kc, nt, preferred_element_type=F32)
        s_n = lax.dot_general(qaug, kn, nt, preferred_element_type=F32)
        s_n = jnp.where(cols <= rows, s_n, -jnp.inf)
        m = jnp.maximum(jnp.max(s_c, axis=1, keepdims=True), jnp.max(s_n, axis=1, keepdims=True))
        p_c = jnp.exp(s_c - m).astype(BF16)
        p_n = jnp.exp(s_n - m).astype(BF16)
        acc = (jnp.dot(p_c, _v_aug(vc_ref[...], hh).astype(BF16), preferred_element_type=F32)
               + jnp.dot(p_n, _v_aug(vn_ref[...], hh).astype(BF16), preferred_element_type=F32))
        accs.append(acc)
    o_ref[...] = _finish_heads(accs[0], accs[1]).astype(o_ref.dtype)


def _fox_attn_sample(q, k_new, v_new, cache_k, cache_v, c_all, b, t):
    n, d = q.shape
    past = cache_k.shape[1]
    nhp = d // LANES
    new_spec = pl.BlockSpec((t, LANES), lambda bi, hp: (bi, hp))
    cache_spec = pl.BlockSpec((None, past, LANES), lambda bi, hp: (bi, 0, hp))
    return pl.pallas_call(
        _fox_attn_sample_kernel,
        grid=(b, nhp),
        in_specs=[new_spec, new_spec, new_spec, cache_spec, cache_spec,
                  pl.BlockSpec((1, past + t, FOX_HEADS), lambda bi, hp: (bi, 0, 0))],
        out_specs=new_spec,
        out_shape=jax.ShapeDtypeStruct((n, d), BF16),
        compiler_params=_params("parallel", "parallel"),
        name="fox_attn_sample",
    )(q, k_new, v_new, cache_k, cache_v, c_all)


def _swiglu_step(h, w1, w3, w2, nsplit=2):
    tf = w1.shape[1]
    g = tf // nsplit
    ups = []
    for s in range(nsplit):
        cs = slice(s * g, (s + 1) * g)
        ups.append((jnp.dot(h, w1[:, cs].astype(BF16), preferred_element_type=F32),
                    jnp.dot(h, w3[:, cs].astype(BF16), preferred_element_type=F32)))
    out = None
    for s, (a, b) in enumerate(ups):
        t = (_silu(a) * b).astype(BF16)
        part = jnp.dot(t, w2[s * g:(s + 1) * g, :].astype(BF16), preferred_element_type=F32)
        out = part if out is None else out + part
    return out


def _ffn_kernel(x_ref, o_ref, wo_ref, g_ref, w1_ref, w3_ref, w2_ref, y_ref, h_s):
    j = pl.program_id(1)

    @pl.when(j == 0)
    def _():
        x1 = x_ref[...] + jnp.dot(o_ref[...], wo_ref[...], preferred_element_type=F32)
        y_ref[...] = x1
        h_s[...] = _rms(x1, g_ref[...]).astype(BF16)

    y_ref[...] += _swiglu_step(h_s[...], w1_ref[...], w3_ref[...], w2_ref[...])


def _attn_out_ffn(x, o, w_o, norm_g, w1, w3, w2, tm, tf):
    n, d = x.shape
    dff = w1.shape[1]
    row = lambda i, j: (i, 0)
    const = lambda i, j: (0, 0)
    return pl.pallas_call(
        _ffn_kernel,
        grid=(n // tm, dff // tf),
        in_specs=[pl.BlockSpec((tm, d), row), pl.BlockSpec((tm, d), row), pl.BlockSpec((d, d), const),
                  pl.BlockSpec((1, d), const), pl.BlockSpec((d, tf), lambda i, j: (0, j)),
                  pl.BlockSpec((d, tf), lambda i, j: (0, j)), pl.BlockSpec((tf, d), lambda i, j: (j, 0))],
        out_specs=pl.BlockSpec((tm, d), row),
        out_shape=jax.ShapeDtypeStruct((n, d), F32),
        scratch_shapes=[pltpu.VMEM((tm, d), BF16)],
        compiler_params=_params("parallel", "arbitrary"),
        name="attn_out_ffn",
    )(x, o, w_o.astype(BF16), norm_g[None, :], w1.astype(BF16), w3.astype(BF16), w2.astype(BF16))


def _gdn_proj_kernel(x_ref, g_ref, w_ref, cw_ref, hist_ref, alog_ref, dtb_ref,
                     q_ref, k_ref, v_ref, gate_ref, aux_ref, hout_ref, pad_s, *, steps_per_seq):
    i = pl.program_id(0)
    tm, d = x_ref.shape
    cdim = 3 * d
    first = (i % steps_per_seq) == 0

    @pl.when(first)
    def _():
        pad_s[0:8, :] = hist_ref[0]

    @pl.when(jnp.logical_not(first))
    def _():
        pad_s[0:8, :] = pad_s[tm:tm + 8, :]

    h = _rms(x_ref[...], g_ref[...])
    y = jnp.dot(h.astype(BF16), w_ref[...], preferred_element_type=F32)
    pad_s[8:8 + tm, :] = y[:, :cdim]
    hout_ref[0] = pad_s[tm:tm + 8, :]

    for blk in range(cdim // LANES):
        cs = slice(blk * LANES, (blk + 1) * LANES)
        mixed = jnp.zeros((tm, LANES), F32)
        for w in range(GDN_CONV):
            mixed = mixed + cw_ref[w:w + 1, cs] * pad_s[8 - (GDN_CONV - 1) + w:8 - (GDN_CONV - 1) + w + tm, cs]
        mixed = _silu(mixed)
        part, hd = divmod(blk, GDN_HEADS)
        hs = slice(hd * LANES, (hd + 1) * LANES)
        if part == 2:
            v_ref[:, hs] = mixed
        else:
            nrm = mixed * lax.rsqrt(jnp.sum(mixed * mixed, axis=1, keepdims=True) + EPS)
            if part == 0:
                q_ref[:, hs] = nrm * (GDN_DH ** -0.5)
            else:
                k_ref[:, hs] = nrm

    gate_ref[...] = _silu(y[:, cdim:cdim + d])
    z = y[:, cdim + d:cdim + d + LANES]
    lane = lax.broadcasted_iota(jnp.int32, z.shape, 1)
    beta = 1.0 / (1.0 + jnp.exp(-z))
    gdec = -jnp.exp(alog_ref[...]) * _softplus(z + dtb_ref[...])
    aux_ref[...] = jnp.where(lane < GDN_HEADS, beta, gdec)


def _gdn_proj(x, hist, b, t, tm, norm_g, w_qkv, conv_w, w_beta, w_a, a_log, dt_bias, w_gate):
    n, d = x.shape
    cdim = 3 * d
    w_all = jnp.concatenate(
        [w_qkv, w_gate, w_beta, w_a, jnp.zeros((d, LANES - 2 * GDN_HEADS), F32)], axis=1).astype(BF16)
    hist8 = jnp.concatenate([jnp.zeros((b, 8 - (GDN_CONV - 1), cdim), F32), hist], axis=1)
    lane_vec = lambda a: jnp.concatenate(
        [jnp.zeros((GDN_HEADS,), F32), a, jnp.zeros((LANES - 2 * GDN_HEADS,), F32)])[None, :]
    steps = t // tm
    row = lambda i: (i, 0)
    const = lambda i: (0, 0)
    outs = pl.pallas_call(
        functools.partial(_gdn_proj_kernel, steps_per_seq=steps),
        grid=(n // tm,),
        in_specs=[pl.BlockSpec((tm, d), row), pl.BlockSpec((1, d), const),
                  pl.BlockSpec((d, 4 * d + LANES), const), pl.BlockSpec((GDN_CONV, cdim), const),
                  pl.BlockSpec((1, 8, cdim), lambda i: (i // steps, 0, 0)),
                  pl.BlockSpec((1, LANES), const), pl.BlockSpec((1, LANES), const)],
        out_specs=[pl.BlockSpec((tm, d), row), pl.BlockSpec((tm, d), row), pl.BlockSpec((tm, d), row),
                   pl.BlockSpec((tm, d), row), pl.BlockSpec((tm, LANES), row),
                   pl.BlockSpec((1, 8, cdim), lambda i: (i // steps, 0, 0))],
        out_shape=[jax.ShapeDtypeStruct((n, d), F32)] * 4 + [jax.ShapeDtypeStruct((n, LANES), F32),
                                                           jax.ShapeDtypeStruct((b, 8, cdim), F32)],
        scratch_shapes=[pltpu.VMEM((tm + 8, cdim), F32)],
        compiler_params=_params("arbitrary"),
        name="gdn_proj",
    )(x, norm_g[None, :], w_all, conv_w, hist8, lane_vec(a_log), lane_vec(dt_bias))
    q, k, v, gate, aux, hout = outs
    return q, k, v, gate, aux, hout[:, 8 - (GDN_CONV - 1):, :]


def _gdn_chunk_kernel(q_ref, k_ref, v_ref, gate_ref, aux_ref, s0_ref, og_ref, o_ref, s_ref, *, chunk):
    ci = pl.program_id(1)
    nb = q_ref.shape[0]
    per = LANES // chunk
    ngroups = GDN_HEADS // per
    nsq = chunk.bit_length() - 2

    @pl.when(ci == 0)
    def _():
        s_ref[...] = s0_ref[...]

    bi = lax.broadcasted_iota(jnp.int32, (LANES, LANES), 0)
    bj = lax.broadcasted_iota(jnp.int32, (LANES, LANES), 1)
    same = (bi // chunk) == (bj // chunk)
    causal = same & (bi >= bj)
    strict = same & (bi > bj)
    eye = (bi == bj).astype(F32)
    tcol = bj // chunk
    nt = (((1,), (1,)), ((), ()))

    chains = []
    for bb in range(nb):
        aux = aux_ref[bb]
        trow = lax.broadcasted_iota(jnp.int32, aux.shape, 0)
        gc = aux
        sh = 1
        while sh < chunk:
            gc = gc + jnp.where(trow >= sh, pltpu.roll(gc, sh, axis=0), 0.0)
            sh *= 2
        glast = gc[chunk - 1:chunk, :]
        for grp in range(ngroups):
            heads = [grp * per + a for a in range(per)]
            stack = lambda ref: jnp.concatenate(
                [ref[bb, :, hd * LANES:(hd + 1) * LANES] for hd in heads], axis=0)
            col = lambda arr, off: jnp.concatenate([arr[:, off + hd:off + hd + 1] for hd in heads], axis=0)
            c = dict(bb=bb, heads=heads, glast=glast)
            c["q2"], c["k2"], c["v2"] = stack(q_ref), stack(k_ref), stack(v_ref)
            c["beta"] = col(aux, 0)
            gcol = col(gc, GDN_HEADS)
            gl = jnp.concatenate([jnp.broadcast_to(glast[:, GDN_HEADS + hd:GDN_HEADS + hd + 1], (chunk, 1))
                                  for hd in heads], axis=0)
            grow = jnp.broadcast_to(gcol, (LANES, LANES)).T
            c["decay"] = jnp.exp(jnp.where(causal, gcol - grow, -jnp.inf))
            c["egc"] = jnp.exp(gcol)
            c["kdec"] = jnp.exp(gl - gcol)
            chains.append(c)

    for c in chains:
        k2b = c["k2"].astype(BF16)
        c["both"] = lax.dot_general(jnp.concatenate([k2b, c["q2"].astype(BF16)], axis=0), k2b, nt,
                                    preferred_element_type=F32)
    for c in chains:
        c["x"] = -jnp.where(strict, c["beta"] * c["both"][:LANES] * c["decay"], 0.0)
        c["attn"] = (c["both"][LANES:] * c["decay"]).astype(BF16)
        xb = c["x"].astype(BF16)
        c["p"] = jnp.dot(xb, xb, preferred_element_type=F32)
        c["tser"] = eye + c["x"]
    for it in range(nsq):
        for c in chains:
            c["pb"] = c["p"].astype(BF16)
            c["tser"] = c["tser"] + jnp.dot(c["tser"].astype(BF16), c["pb"], preferred_element_type=F32)
        if it + 1 < nsq:
            for c in chains:
                c["p"] = jnp.dot(c["pb"], c["pb"], preferred_element_type=F32)
    for c in chains:
        rhs = jnp.concatenate([c["v2"] * c["beta"], c["k2"] * (c["beta"] * c["egc"])], axis=1)
        c["sol"] = rhs + jnp.dot((c["tser"] - eye).astype(BF16), rhs.astype(BF16), preferred_element_type=F32)
    for c in chains:
        w2b = c["sol"][:, LANES:].astype(BF16)
        qd = (c["q2"] * c["egc"]).astype(BF16)
        c["wq"] = []
        for a, hd in enumerate(c["heads"]):
            rs = slice(a * chunk, (a + 1) * chunk)
            c["wq"].append(jnp.dot(jnp.concatenate([w2b[rs], qd[rs]], axis=0), s_ref[c["bb"], hd].astype(BF16),
                                   preferred_element_type=F32))
    for c in chains:
        ws = jnp.concatenate([wq[:chunk] for wq in c["wq"]], axis=0)
        c["qs"] = jnp.concatenate([wq[chunk:] for wq in c["wq"]], axis=0)
        c["vnb"] = (c["sol"][:, :LANES] - ws).astype(BF16)
        c["kdt"] = (c["k2"] * c["kdec"]).T
    for c in chains:
        c["o2"] = c["qs"] + jnp.dot(c["attn"], c["vnb"], preferred_element_type=F32)
        for a, hd in enumerate(c["heads"]):
            kdt_h = jnp.where(tcol == a, c["kdt"], 0.0).astype(BF16)
            decay_s = jnp.exp(c["glast"][:, GDN_HEADS + hd:GDN_HEADS + hd + 1])
            s_ref[c["bb"], hd] = (s_ref[c["bb"], hd] * decay_s
                                  + jnp.dot(kdt_h, c["vnb"], preferred_element_type=F32))
    for c in chains:
        for a, hd in enumerate(c["heads"]):
            rs = slice(a * chunk, (a + 1) * chunk)
            oh = _rms(c["o2"][rs], og_ref[...]) * gate_ref[c["bb"], :, hd * LANES:(hd + 1) * LANES]
            o_ref[c["bb"], :, hd * LANES:(hd + 1) * LANES] = oh.astype(o_ref.dtype)


def _gdn_chunks(q, k, v, gate, aux, s0, out_gain, b, t, chunk, nb):
    n, d = q.shape
    nc = t // chunk
    seq = lambda a: a.reshape(b, t, a.shape[1])
    row = lambda bi, ci: (bi, ci, 0)
    st = lambda bi, ci: (bi, 0, 0, 0)
    sshape = (nb, GDN_HEADS, GDN_DH, GDN_DH)
    o, s_new = pl.pallas_call(
        functools.partial(_gdn_chunk_kernel, chunk=chunk),
        grid=(b // nb, nc),
        in_specs=[pl.BlockSpec((nb, chunk, d), row)] * 4 + [pl.BlockSpec((nb, chunk, LANES), row),
                                                           pl.BlockSpec(sshape, st),
                                                           pl.BlockSpec((1, GDN_DH), lambda bi, ci: (0, 0))],
        out_specs=[pl.BlockSpec((nb, chunk, d), row), pl.BlockSpec(sshape, st)],
        out_shape=[jax.ShapeDtypeStruct((b, t, d), BF16), jax.ShapeDtypeStruct(s0.shape, F32)],
        compiler_params=_params("parallel", "arbitrary"),
        name="gdn_chunks",
    )(seq(q), seq(k), seq(v), seq(gate), seq(aux), s0, out_gain[None, :])
    return o.reshape(n, d), s_new


def _router_kernel(x_ref, o_ref, wo_ref, g_ref, r_ref, base_ref, x1_ref, h3_ref, info_ref, cnt_ref, cnt_s,
                   *, cap):
    i = pl.program_id(0)
    tm, d = x_ref.shape

    @pl.when(i == 0)
    def _():
        cnt_s[...] = base_ref[...]

    x1 = x_ref[...] + jnp.dot(o_ref[...], wo_ref[...], preferred_element_type=F32)
    x1_ref[...] = x1
    h = _rms(x1, g_ref[...])
    nchunk = d // LANES
    for c in range(nchunk):
        h3_ref[pl.ds(c, tm, stride=nchunk), :] = h[:, c * LANES:(c + 1) * LANES]
    logits = jnp.dot(h, r_ref[...], preferred_element_type=F32, precision=HIGHEST)
    lane = lax.broadcasted_iota(jnp.int32, logits.shape, 1)
    logits = jnp.where(lane < N_EXPERTS, logits, -jnp.inf)
    m1 = jnp.max(logits, axis=1, keepdims=True)
    i1 = jnp.min(jnp.where(logits == m1, lane, LANES), axis=1, keepdims=True)
    rest = jnp.where(lane == i1, -jnp.inf, logits)
    m2 = jnp.max(rest, axis=1, keepdims=True)
    i2 = jnp.min(jnp.where(rest == m2, lane, LANES), axis=1, keepdims=True)
    e2 = jnp.exp(m2 - m1)
    g1 = 1.0 / (1.0 + e2)
    g2 = e2 / (1.0 + e2)
    member = ((lane == i1) | (lane == i2)).astype(F32)
    ri = lax.broadcasted_iota(jnp.int32, (tm, tm), 0)
    rj = lax.broadcasted_iota(jnp.int32, (tm, tm), 1)
    rank = jnp.dot((rj < ri).astype(BF16), member.astype(BF16), preferred_element_type=F32)
    slot = cnt_s[...] + rank + lane.astype(F32) * float(cap)
    d1 = jnp.sum(jnp.where(lane == i1, slot, 0.0), axis=1, keepdims=True)
    d2 = jnp.sum(jnp.where(lane == i2, slot, 0.0), axis=1, keepdims=True)
    cnt_s[...] += jnp.sum(member, axis=0, keepdims=True)
    cnt_ref[...] = cnt_s[...]
    info = jnp.zeros(logits.shape, F32)
    for k, val in enumerate((i1.astype(F32), i2.astype(F32), g1, g2, d1, d2)):
        info = jnp.where(lane == k, val, info)
    info_ref[...] = info


def _moe_router(x, o, w_o, norm_g, router, base_counts, tm, cap):
    n, d = x.shape
    r_pad = jnp.concatenate([router, jnp.zeros((d, LANES - N_EXPERTS), F32)], axis=1)
    row = lambda i: (i, 0)
    const = lambda i: (0, 0)
    x1, h3, info, counts = pl.pallas_call(
        functools.partial(_router_kernel, cap=cap),
        grid=(n // tm,),
        in_specs=[pl.BlockSpec((tm, d), row), pl.BlockSpec((tm, d), row), pl.BlockSpec((d, d), const),
                  pl.BlockSpec((1, d), const), pl.BlockSpec((d, LANES), const), pl.BlockSpec((1, LANES), const)],
        out_specs=[pl.BlockSpec((tm, d), row), pl.BlockSpec((tm * (d // LANES), LANES), row),
                   pl.BlockSpec((tm, LANES), row), pl.BlockSpec((1, LANES), const)],
        out_shape=[jax.ShapeDtypeStruct((n, d), F32), jax.ShapeDtypeStruct((n * (d // LANES), LANES), F32),
                   jax.ShapeDtypeStruct((n, LANES), F32), jax.ShapeDtypeStruct((1, LANES), F32)],
        scratch_shapes=[pltpu.VMEM((1, LANES), F32)],
        compiler_params=_params("arbitrary"),
        name="moe_router",
    )(x, o, w_o.astype(BF16), norm_g[None, :], r_pad, base_counts)
    return x1, h3.reshape(n, d // LANES, LANES), info, counts


def _row_dests(info, tm):
    n = info.shape[0]
    dest = info[:, 4:6].astype(jnp.int32)
    return dest.reshape(n // tm, tm, 2).transpose(0, 2, 1).reshape(n // tm, 1, 2 * tm)


def _dispatch_kernel(*refs, tm, final, tile):
    if final:
        dest_ref, pad_ref, h3_ref, _, xs_ref, sem, zeros_s = refs
    else:
        dest_ref, h3_ref, xs_ref, sem = refs
    i = pl.program_id(0)
    nchunk = xs_ref.shape[1]

    def copies(r):
        src = h3_ref.at[pl.ds(pl.multiple_of(r * nchunk, nchunk), nchunk)]
        return (pltpu.make_async_copy(src, xs_ref.at[dest_ref[0, 0, r]], sem),
                pltpu.make_async_copy(src, xs_ref.at[dest_ref[0, 0, tm + r]], sem))

    def issue(r, carry):
        for cp in copies(r):
            cp.start()
        return carry

    lax.fori_loop(0, tm, issue, 0, unroll=8)
    landed = xs_ref.at[pl.ds(0, 2 * tm)]
    pltpu.make_async_copy(landed, landed, sem).wait()

    if final:
        @pl.when(i == pl.num_programs(0) - 1)
        def _():
            zeros_s[...] = jnp.zeros(zeros_s.shape, F32)
            for e in range(N_EXPERTS):
                start = pad_ref[0, e]
                tail = pad_ref[1, e]
                for bit in reversed(range((tile - 1).bit_length())):
                    size = 1 << bit
                    off = start + ((tail >> (bit + 1)) << (bit + 1))

                    @pl.when(((tail >> bit) & 1) == 1)
                    def _():
                        cp = pltpu.make_async_copy(zeros_s.at[pl.ds(0, size)], xs_ref.at[pl.ds(off, size)], sem)
                        cp.start()
                        cp.wait()


def _moe_dispatch(h3, dest, tm, rows, xs=None, pad=None, tile=None):
    n, s, _ = h3.shape
    final = xs is not None
    smem = functools.partial(pl.BlockSpec, memory_space=pltpu.SMEM)
    anyspec = pl.BlockSpec(memory_space=pl.ANY)
    in_specs = [smem((1, 1, 2 * tm), lambda i: (i, 0, 0))]
    args = [dest]
    if final:
        in_specs.append(smem((2, N_EXPERTS), lambda i: (0, 0)))
        args.append(pad)
    in_specs.append(pl.BlockSpec((tm * s, LANES), lambda i: (i, 0)))
    args.append(h3.reshape(n * s, LANES))
    scratch = [pltpu.SemaphoreType.DMA(())]
    if final:
        in_specs.append(anyspec)
        args.append(xs)
        scratch.append(pltpu.VMEM((1 << ((tile - 1).bit_length() - 1), s, LANES), F32))
    return pl.pallas_call(
        functools.partial(_dispatch_kernel, tm=tm, final=final, tile=tile),
        grid=(n // tm,),
        in_specs=in_specs,
        out_specs=anyspec,
        out_shape=jax.ShapeDtypeStruct((rows, s, LANES), F32),
        scratch_shapes=scratch,
        input_output_aliases={len(args) - 1: 0} if final else {},
        compiler_params=_params("arbitrary"),
        name="moe_dispatch",
    )(*args)


def _experts_kernel(blk_ref, exp_ref, valid_ref, xs_ref, w1_ref, w3_ref, w2_ref, ys_ref, h_s, acc_s):
    t = pl.program_id(0)
    j = pl.program_id(1)
    tile, d = h_s.shape
    nchunk = d // LANES

    @pl.when(valid_ref[t] == 1)
    def _():
        @pl.when(j == 0)
        def _():
            for c in range(nchunk):
                h_s[:, c * LANES:(c + 1) * LANES] = xs_ref[pl.ds(c, tile, stride=nchunk), :].astype(BF16)

        part = _swiglu_step(h_s[...], w1_ref[0], w3_ref[0], w2_ref[0])

        @pl.when(j == 0)
        def _():
            acc_s[...] = part

        @pl.when(j > 0)
        def _():
            acc_s[...] += part

        @pl.when(j == pl.num_programs(1) - 1)
        def _():
            for c in range(nchunk):
                ys_ref[pl.ds(c, tile, stride=nchunk), :] = acc_s[:, c * LANES:(c + 1) * LANES]


def _moe_experts(xs, counts, w1, w3, w2, cap, tile, tf, n_assign):
    rows, s, _ = xs.shape
    ne, d, dff = w1.shape
    nj = dff // tf
    n_tiles = -(-n_assign // tile) + ne
    tiles_e = (counts + tile - 1) // tile
    cum = jnp.cumsum(tiles_e)
    total = cum[-1]
    t = jnp.arange(n_tiles, dtype=jnp.int32)
    t_eff = jnp.maximum(jnp.minimum(t, total - 1), 0)
    e_of_t = jnp.minimum(jnp.sum(t_eff[:, None] >= cum[None, :], axis=1), ne - 1).astype(jnp.int32)
    blk = (e_of_t * (cap // tile) + t_eff - (cum - tiles_e)[e_of_t]).astype(jnp.int32)
    valid = (t < total).astype(jnp.int32)
    jmap = lambda j, v: j * v + (nj - 1) * (1 - v)
    grid_spec = pltpu.PrefetchScalarGridSpec(
        num_scalar_prefetch=3,
        grid=(n_tiles, nj),
        in_specs=[pl.BlockSpec((tile * s, LANES), lambda t, j, blk, ex, va: (blk[t], 0)),
                  pl.BlockSpec((1, d, tf), lambda t, j, blk, ex, va: (ex[t], 0, jmap(j, va[t]))),
                  pl.BlockSpec((1, d, tf), lambda t, j, blk, ex, va: (ex[t], 0, jmap(j, va[t]))),
                  pl.BlockSpec((1, tf, d), lambda t, j, blk, ex, va: (ex[t], jmap(j, va[t]), 0))],
        out_specs=pl.BlockSpec((tile * s, LANES), lambda t, j, blk, ex, va: (blk[t], 0)),
        scratch_shapes=[pltpu.VMEM((tile, d), BF16), pltpu.VMEM((tile, d), F32)])
    ys = pl.pallas_call(
        _experts_kernel,
        grid_spec=grid_spec,
        out_shape=jax.ShapeDtypeStruct((rows * s, LANES), F32),
        compiler_params=_params("arbitrary", "arbitrary"),
        name="moe_experts",
    )(blk, e_of_t, valid, xs.reshape(rows * s, LANES), w1, w3, w2)
    return ys.reshape(rows, s, LANES)


def _combine_kernel(dest_ref, x1_ref, info_ref, ys_ref, y_ref, buf, sem):
    tm, d = x1_ref.shape
    nchunk = d // LANES

    def slab(k):
        return buf.at[pl.ds(pl.multiple_of(k * nchunk, nchunk), nchunk)]

    def copies(r):
        return (pltpu.make_async_copy(ys_ref.at[dest_ref[0, 0, r]], slab(r), sem),
                pltpu.make_async_copy(ys_ref.at[dest_ref[0, 0, tm + r]], slab(tm + r), sem))

    def issue(r, carry):
        for cp in copies(r):
            cp.start()
        return carry

    lax.fori_loop(0, tm, issue, 0, unroll=8)
    pltpu.make_async_copy(buf, buf, sem).wait()
    g1 = info_ref[:, 2:3]
    g2 = info_ref[:, 3:4]
    for c in range(nchunk):
        cs = slice(c * LANES, (c + 1) * LANES)
        r1 = buf[pl.ds(c, tm, stride=nchunk), :]
        r2 = buf[pl.ds(tm * nchunk + c, tm, stride=nchunk), :]
        y_ref[:, cs] = x1_ref[:, cs] + g1 * r1 + g2 * r2


def _moe_combine(x1, info, dest, ys, tm):
    n, d = x1.shape
    s = d // LANES
    row = lambda i: (i, 0)
    return pl.pallas_call(
        _combine_kernel,
        grid=(n // tm,),
        in_specs=[pl.BlockSpec((1, 1, 2 * tm), lambda i: (i, 0, 0), memory_space=pltpu.SMEM),
                  pl.BlockSpec((tm, d), row), pl.BlockSpec((tm, LANES), row),
                  pl.BlockSpec(memory_space=pl.ANY)],
        out_specs=pl.BlockSpec((tm, d), row),
        out_shape=jax.ShapeDtypeStruct((n, d), F32),
        scratch_shapes=[pltpu.VMEM((2 * tm * s, LANES), F32), pltpu.SemaphoreType.DMA(())],
        compiler_params=_params("arbitrary"),
        name="moe_combine",
    )(dest, x1, info, ys)


def _moe_block(streams, moe_w, w_o):
    moe_norm, router, w1, w3, w2 = moe_w
    n_total = sum(x.shape[0] for x, _ in streams)
    share = -(-2 * n_total * 11 // (10 * N_EXPERTS))
    tile = -(-share // (MOE_TILES_PER_EXPERT * LANES)) * LANES
    cap = -(-n_total // tile) * tile
    rows = N_EXPERTS * cap
    counts = jnp.zeros((1, LANES), F32)
    routed = []
    for x, o in streams:
        tm = min(512, x.shape[0])
        x1, h3, info, counts = _moe_router(x, o, w_o, moe_norm, router, counts, tm, cap)
        routed.append((x1, h3, info, _row_dests(info, tm), tm))
    cnt = counts[0, :N_EXPERTS].astype(jnp.int32)
    pad = jnp.stack([jnp.arange(N_EXPERTS, dtype=jnp.int32) * cap + cnt, (-cnt) % tile])
    xs = None
    for k, (x1, h3, info, dest, tm) in enumerate(routed):
        if k + 1 < len(routed):
            assert xs is None, "only the first and the final dispatch are distinguished"
            xs = _moe_dispatch(h3, dest, tm, rows)
        else:
            xs = _moe_dispatch(h3, dest, tm, rows, xs=xs, pad=pad, tile=tile)
    ys = _moe_experts(xs, cnt, w1, w3, w2, cap, tile, MOE_TF, 2 * n_total)
    return [_moe_combine(x1, info, dest, ys, tm) for x1, h3, info, dest, tm in routed]


def _fox_layer(x, b, t, cache, w, ffn_w):
    n, d = x.shape
    tm = min(512, n)
    q, k, v, logf = _fox_proj(x, *w[:6], tm=tm)
    if cache is None:
        c = _cumsum_time(logf.reshape(b, t, FOX_HEADS))
        o = _fox_attn_prompt(q, k, v, c, b, t, tq=min(512, t), tk=min(512, t))
    else:
        ck, cv, cl = cache
        past = ck.shape[1]
        c = _cumsum_time(jnp.concatenate([cl, logf.reshape(b, t, FOX_HEADS)], axis=1))
        o = _fox_attn_sample(q, k, v, ck.reshape(b, past, d), cv.reshape(b, past, d), c, b, t)
    y = _attn_out_ffn(x, o, w[6], *ffn_w, tm=min(1024, n), tf=512)
    return y, k, v, logf


def _gdn_mixer(x, b, t, chunk, hist, s0, w):
    norm_g, w_qkv, conv_w, w_beta, w_a, a_log, dt_bias, w_gate, out_gain, _ = w
    tm = min(512, t)
    q, k, v, gate, aux, new_hist = _gdn_proj(x, hist, b, t, tm, norm_g, w_qkv, conv_w, w_beta, w_a,
                                             a_log, dt_bias, w_gate)
    groups_per_seq = GDN_HEADS // (LANES // chunk)
    nb = max(1, min(b, GDN_CHAINS_PER_STEP // groups_per_seq))
    o, s_new = _gdn_chunks(q, k, v, gate, aux, s0, out_gain, b, t, chunk, nb)
    return o, s_new, new_hist


def kernel(x_prompt, x_sample, cache_fox_k, cache_fox_v, cache_fox_logf, state_gdn_s, state_gdn_conv, fox_norm, fox_w_qkv, fox_q_gain, fox_k_gain, fox_w_f, fox_b_f, fox_w_o, ffn_norm, ffn_w1, ffn_w3, ffn_w2, gdn_norm, gdn_w_qkv, gdn_conv_w, gdn_w_beta, gdn_w_a, gdn_a_log, gdn_dt_bias, gdn_w_gate, gdn_out_gain, gdn_w_o, moe_norm, moe_router, moe_w1, moe_w3, moe_w2):
    bp, tp, d = x_prompt.shape
    bs, ts, _ = x_sample.shape
    depth = fox_norm.shape[0] + gdn_norm.shape[0]
    cdim = gdn_w_qkv.shape[2]
    yp = x_prompt.reshape(bp * tp, d)
    ys = x_sample.reshape(bs * ts, d)
    kp_l, vp_l, lp_l, ks_l, vs_l, ls_l = [], [], [], [], [], []
    sp_l, cp_l, ss_l, cs_l = [], [], [], []
    for i in range(depth):
        j = i // 2
        if i % 2 == 0:
            fw = (fox_norm[j], fox_w_qkv[j], fox_q_gain[j], fox_k_gain[j], fox_w_f[j], fox_b_f[j], fox_w_o[j])
            ffw = (ffn_norm[j], ffn_w1[j], ffn_w3[j], ffn_w2[j])
            yp, kp, vp, lp = _fox_layer(yp, bp, tp, None, fw, ffw)
            ys, ks, vs, ls = _fox_layer(ys, bs, ts, (cache_fox_k[j], cache_fox_v[j], cache_fox_logf[j]), fw, ffw)
            kp_l.append(kp.reshape(bp, tp, FOX_HEADS, FOX_DH)); vp_l.append(vp.reshape(bp, tp, FOX_HEADS, FOX_DH))
            lp_l.append(lp.reshape(bp, tp, FOX_HEADS))
            ks_l.append(ks.reshape(bs, ts, FOX_HEADS, FOX_DH)); vs_l.append(vs.reshape(bs, ts, FOX_HEADS, FOX_DH))
            ls_l.append(ls.reshape(bs, ts, FOX_HEADS))
        else:
            gw = (gdn_norm[j], gdn_w_qkv[j], gdn_conv_w[j], gdn_w_beta[j], gdn_w_a[j], gdn_a_log[j],
                  gdn_dt_bias[j], gdn_w_gate[j], gdn_out_gain[j], gdn_w_o[j])
            mw = (moe_norm[j], moe_router[j], moe_w1[j], moe_w3[j], moe_w2[j])
            hist0 = jnp.zeros((bp, GDN_CONV - 1, cdim), F32)
            s_zero = jnp.zeros((bp, GDN_HEADS, GDN_DH, GDN_DH), F32)
            op, sp, cp = _gdn_mixer(yp, bp, tp, 64, hist0, s_zero, gw)
            os_, ss, cs = _gdn_mixer(ys, bs, ts, ts, state_gdn_conv[j], state_gdn_s[j], gw)
            yp, ys = _moe_block([(yp, op), (ys, os_)], mw, gw[-1])
            sp_l.append(sp); cp_l.append(cp); ss_l.append(ss); cs_l.append(cs)
    return (yp.reshape(bp, tp, d), ys.reshape(bs, ts, d),
            jnp.stack(kp_l), jnp.stack(vp_l), jnp.stack(lp_l),
            jnp.stack(ks_l), jnp.stack(vs_l), jnp.stack(ls_l),
            jnp.stack(sp_l), jnp.stack(cp_l), jnp.stack(ss_l), jnp.stack(cs_l))
```

```python
import functools

import jax
import jax.numpy as jnp
from jax import lax
from jax.experimental import pallas as pl
from jax.experimental.pallas import tpu as pltpu

F32 = jnp.float32
BF16 = jnp.bfloat16
EPS = 1e-6
LANES = 128
VMEM_LIMIT = 56 * 1024 * 1024

FOX_HEADS = 16
FOX_DH = 64
GDN_HEADS = 8
GDN_DH = 128
GDN_CONV = 4
N_EXPERTS = 8
MOE_TILES_PER_EXPERT = 4
MOE_TF = 512
GDN_CHAINS_PER_STEP = 16
HIGHEST = lax.Precision.HIGHEST


def _params(*sem):
    return pltpu.CompilerParams(dimension_semantics=sem, vmem_limit_bytes=VMEM_LIMIT)


def _rms(x, gain):
    return x * lax.rsqrt(jnp.mean(x * x, axis=-1, keepdims=True) + EPS) * gain


def _silu(x):
    return x * (1.0 / (1.0 + jnp.exp(-x)))


def _softplus(x):
    return jnp.maximum(x, 0.0) + jnp.log(1.0 + jnp.exp(-jnp.abs(x)))


def _split3(c):
    hi = c.astype(BF16).astype(F32)
    r = c - hi
    mid = r.astype(BF16).astype(F32)
    lo = (r - mid).astype(BF16).astype(F32)
    return hi, mid, lo


def _fox_proj_kernel(x_ref, g_ref, w_ref, qg_ref, kg_ref, bf_ref, gm_ref, gmt_ref,
                     q_ref, k_ref, v_ref, lf_ref):
    d = x_ref.shape[1]
    h = _rms(x_ref[...], g_ref[...])
    y = jnp.dot(h.astype(BF16), w_ref[...], preferred_element_type=F32)

    def headnorm(z, gain):
        ss = jnp.dot((z * z).astype(BF16), gm_ref[...], preferred_element_type=F32)
        inv = lax.rsqrt(ss * (1.0 / FOX_DH) + EPS)
        hi = inv.astype(BF16)
        lo = (inv - hi.astype(F32)).astype(BF16)
        full = (jnp.dot(hi, gmt_ref[...], preferred_element_type=F32)
                + jnp.dot(lo, gmt_ref[...], preferred_element_type=F32))
        return z * full * gain

    q_ref[...] = (headnorm(y[:, :d], qg_ref[...]) * (FOX_DH ** -0.5)).astype(q_ref.dtype)
    k_ref[...] = headnorm(y[:, d:2 * d], kg_ref[...])
    v_ref[...] = y[:, 2 * d:3 * d]
    z = y[:, 3 * d:3 * d + LANES] + bf_ref[...]
    lf = jnp.minimum(z, 0.0) - jnp.log(1.0 + jnp.exp(-jnp.abs(z)))
    lf_ref[...] = lf[:, :FOX_HEADS]


def _fox_proj(x, norm_g, w_qkv, q_gain, k_gain, w_f, b_f, tm):
    n, d = x.shape
    w_all = jnp.concatenate(
        [w_qkv, w_f, jnp.zeros((d, LANES - FOX_HEADS), F32)], axis=1).astype(BF16)
    bf = jnp.concatenate([b_f, jnp.zeros((LANES - FOX_HEADS,), F32)])[None, :]
    head_of_col = jnp.arange(d) // FOX_DH
    gm = (head_of_col[:, None] == jnp.arange(LANES)[None, :]).astype(BF16)
    gmt = gm.T
    qg = jnp.tile(q_gain, FOX_HEADS)[None, :]
    kg = jnp.tile(k_gain, FOX_HEADS)[None, :]
    const = lambda i: (0, 0)
    row = lambda i: (i, 0)
    return pl.pallas_call(
        _fox_proj_kernel,
        grid=(n // tm,),
        in_specs=[pl.BlockSpec((tm, d), row), pl.BlockSpec((1, d), const),
                  pl.BlockSpec((d, 3 * d + LANES), const), pl.BlockSpec((1, d), const),
                  pl.BlockSpec((1, d), const), pl.BlockSpec((1, LANES), const),
                  pl.BlockSpec((d, LANES), const), pl.BlockSpec((LANES, d), const)],
        out_specs=[pl.BlockSpec((tm, d), row), pl.BlockSpec((tm, d), row), pl.BlockSpec((tm, d), row),
                   pl.BlockSpec((tm, FOX_HEADS), row)],
        out_shape=[jax.ShapeDtypeStruct((n, d), BF16), jax.ShapeDtypeStruct((n, d), F32),
                   jax.ShapeDtypeStruct((n, d), F32), jax.ShapeDtypeStruct((n, FOX_HEADS), F32)],
        compiler_params=_params("parallel"),
        name="fox_proj",
    )(x, norm_g[None, :], w_all, qg, kg, bf, gm, gmt)


def _cumsum_kernel(x_ref, o_ref):
    x = x_ref[0]
    t = x.shape[0]
    row = lax.broadcasted_iota(jnp.int32, x.shape, 0)
    s = 1
    while s < t:
        x = x + jnp.where(row >= s, pltpu.roll(x, s, axis=0), 0.0)
        s *= 2
    o_ref[0] = x


def _cumsum_time(x):
    b, t, hn = x.shape
    spec = pl.BlockSpec((1, t, hn), lambda i: (i, 0, 0))
    return pl.pallas_call(
        _cumsum_kernel, grid=(b,), in_specs=[spec], out_specs=spec,
        out_shape=jax.ShapeDtypeStruct(x.shape, F32),
        compiler_params=_params("parallel"), name="logf_cumsum",
    )(x)


def _head_col(c16, head):
    lane = lax.broadcasted_iota(jnp.int32, c16.shape, 1)
    return jnp.sum(jnp.where(lane == head, c16, 0.0), axis=1, keepdims=True)


def _aug(data, ccol, hh, is_q):
    lane = lax.broadcasted_iota(jnp.int32, data.shape, 1)
    base = FOX_DH if hh == 0 else 0
    in_data = (lane < FOX_DH) if hh == 0 else (lane >= FOX_DH)
    hi, mid, lo = _split3(ccol)
    lane1 = lax.broadcasted_iota(jnp.int32, (1, data.shape[1]), 1)
    if is_q:
        pieces = {0: hi, 1: mid, 2: lo}
        ones = (lane1 >= base + 3) & (lane1 < base + 6)
    else:
        pieces = {3: -hi, 4: -mid, 5: -lo}
        ones = (lane1 >= base) & (lane1 < base + 3)
    aug = jnp.where(ones, 1.0, 0.0)
    for i, p in pieces.items():
        aug = jnp.where(lane == base + i, p, aug)
    return jnp.where(in_data, data, aug)


def _v_aug(v2, hh):
    lane = lax.broadcasted_iota(jnp.int32, v2.shape, 1)
    in_data = (lane < FOX_DH) if hh == 0 else (lane >= FOX_DH)
    return jnp.where(in_data, v2, 1.0)


def _finish_heads(acc0, acc1):
    lane = lax.broadcasted_iota(jnp.int32, acc0.shape, 1)
    o0 = acc0 / pltpu.roll(acc0, FOX_DH, axis=1)
    o1 = acc1 / pltpu.roll(acc1, FOX_DH, axis=1)
    return jnp.where(lane < FOX_DH, o0, o1)


def _fox_attn_kernel(q_ref, k_ref, v_ref, c_ref, o_ref, kt_s, v_s, *, tq, tk):
    hp = pl.program_id(1)
    t = k_ref.shape[0]
    nkb = t // tk
    nd = tq // tk
    c16 = c_ref[0]
    k2 = k_ref[...]
    v2 = v_ref[...]
    ccols = [_head_col(c16, 2 * hp + hh) for hh in range(2)]
    for hh in range(2):
        kaug = _aug(k2, ccols[hh], hh, False)
        for jb in range(nkb):
            kt_s[hh, jb] = kaug[jb * tk:(jb + 1) * tk, :].T.astype(BF16)
        v_s[hh] = _v_aug(v2, hh).astype(BF16)

    rows = lax.broadcasted_iota(jnp.int32, (tq, tk), 0)
    cols = lax.broadcasted_iota(jnp.int32, (tq, tk), 1)
    chains = []
    for i in range(t // tq):
        q2 = q_ref[i * tq:(i + 1) * tq, :].astype(F32)
        for hh in range(2):
            qb = _aug(q2, ccols[hh][i * tq:(i + 1) * tq], hh, True).astype(BF16)
            order = list(range(i * nd, (i + 1) * nd)) + list(range(i * nd))
            chains.append(dict(i=i, hh=hh, qb=qb, order=order, m=None, acc=None))
    for rnd in range(max(len(c["order"]) for c in chains)):
        live = [c for c in chains if rnd < len(c["order"])]
        for c in live:
            j = c["order"][rnd]
            s = jnp.dot(c["qb"], kt_s[c["hh"], j], preferred_element_type=F32)
            dblk = j - c["i"] * nd
            if dblk >= 0:
                s = jnp.where(cols + dblk * tk <= rows, s, -jnp.inf)
            c["s"] = s
        for c in live:
            bm = jnp.max(c["s"], axis=1, keepdims=True)
            m_new = bm if c["m"] is None else jnp.maximum(c["m"], bm)
            c["p"] = jnp.exp(c["s"] - m_new).astype(BF16)
            c["alpha"] = None if c["m"] is None else jnp.exp(c["m"] - m_new)
            c["m"] = m_new
        for c in live:
            j = c["order"][rnd]
            pv = jnp.dot(c["p"], v_s[c["hh"], j * tk:(j + 1) * tk, :], preferred_element_type=F32)
            c["acc"] = pv if c["acc"] is None else c["alpha"] * c["acc"] + pv
    for i in range(t // tq):
        o_ref[i * tq:(i + 1) * tq, :] = _finish_heads(chains[2 * i]["acc"],
                                                       chains[2 * i + 1]["acc"]).astype(o_ref.dtype)


def _fox_attn_prompt(q, k, v, c, b, t, tq, tk):
    n, d = q.shape
    nhp = d // LANES
    blk = pl.BlockSpec((t, LANES), lambda bi, hp: (bi, hp))
    return pl.pallas_call(
        functools.partial(_fox_attn_kernel, tq=tq, tk=tk),
        grid=(b, nhp),
        in_specs=[blk, blk, blk, pl.BlockSpec((1, t, FOX_HEADS), lambda bi, hp: (bi, 0, 0))],
        out_specs=blk,
        out_shape=jax.ShapeDtypeStruct((n, d), BF16),
        scratch_shapes=[pltpu.VMEM((2, t // tk, LANES, tk), BF16), pltpu.VMEM((2, t, LANES), BF16)],
        compiler_params=_params("parallel", "parallel"),
        name="fox_attn_prompt",
    )(q, k, v, c)


def _fox_attn_sample_kernel(q_ref, kn_ref, vn_ref, kc_ref, vc_ref, c_ref, o_ref):
    t, d = q_ref.shape
    past = kc_ref.shape[0]
    c16 = c_ref[0]
    rows = lax.broadcasted_iota(jnp.int32, (t, t), 0)
    cols = lax.broadcasted_iota(jnp.int32, (t, t), 1)
    nt = (((1,), (1,)), ((), ()))
    for hp in range(d // LANES):
        ls = slice(hp * LANES, (hp + 1) * LANES)
        q2 = q_ref[:, ls].astype(F32)
        accs = []
        for hh in range(2):
            ccol = _head_col(c16, 2 * hp + hh)
            qaug = _aug(q2, ccol[past:], hh, True).astype(BF16)
            kc = _aug(kc_ref[:, ls], ccol[:past], hh, False).astype(BF16)
            kn = _aug(kn_ref[:, ls], ccol[past:], hh, False).astype(BF16)
            s_c = lax.dot_general(qaug, kc, nt, preferred_element_type=F32)
            s_n = lax.dot_general(qaug, kn, nt, preferred_element_type=F32)
            s_n = jnp.where(cols <= rows, s_n, -jnp.inf)
            m = jnp.maximum(jnp.max(s_c, axis=1, keepdims=True), jnp.max(s_n, axis=1, keepdims=True))
            p_c = jnp.exp(s_c - m).astype(BF16)
            p_n = jnp.exp(s_n - m).astype(BF16)
            acc = (jnp.dot(p_c, _v_aug(vc_ref[:, ls], hh).astype(BF16), preferred_element_type=F32)
                   + jnp.dot(p_n, _v_aug(vn_ref[:, ls], hh).astype(BF16), preferred_element_type=F32))
            accs.append(acc)
        o_ref[:, ls] = _finish_heads(accs[0], accs[1]).astype(o_ref.dtype)


def _fox_attn_sample(q, k_new, v_new, cache_k, cache_v, c_all, b, t):
    n, d = q.shape
    past = cache_k.shape[1]
    new_spec = pl.BlockSpec((t, d), lambda bi: (bi, 0))
    cache_spec = pl.BlockSpec((None, past, d), lambda bi: (bi, 0, 0))
    return pl.pallas_call(
        _fox_attn_sample_kernel,
        grid=(b,),
        in_specs=[new_spec, new_spec, new_spec, cache_spec, cache_spec,
                  pl.BlockSpec((1, past + t, FOX_HEADS), lambda bi: (bi, 0, 0))],
        out_specs=new_spec,
        out_shape=jax.ShapeDtypeStruct((n, d), BF16),
        compiler_params=_params("parallel"),
        name="fox_attn_sample",
    )(q, k_new, v_new, cache_k, cache_v, c_all)


def _swiglu_step(h, w1, w3, w2, nsplit=2):
    tf = w1.shape[1]
    g = tf // nsplit
    ups = []
    for s in range(nsplit):
        cs = slice(s * g, (s + 1) * g)
        ups.append((jnp.dot(h, w1[:, cs].astype(BF16), preferred_element_type=F32),
                    jnp.dot(h, w3[:, cs].astype(BF16), preferred_element_type=F32)))
    out = None
    for s, (a, b) in enumerate(ups):
        t = (_silu(a) * b).astype(BF16)
        part = jnp.dot(t, w2[s * g:(s + 1) * g, :].astype(BF16), preferred_element_type=F32)
        out = part if out is None else out + part
    return out


def _ffn_kernel(x_ref, o_ref, wo_ref, g_ref, w1_ref, w3_ref, w2_ref, y_ref, h_s):
    j = pl.program_id(1)

    @pl.when(j == 0)
    def _():
        x1 = x_ref[...] + jnp.dot(o_ref[...], wo_ref[...], preferred_element_type=F32)
        y_ref[...] = x1
        h_s[...] = _rms(x1, g_ref[...]).astype(BF16)

    y_ref[...] += _swiglu_step(h_s[...], w1_ref[...], w3_ref[...], w2_ref[...])


def _attn_out_ffn(x, o, w_o, norm_g, w1, w3, w2, tm, tf):
    n, d = x.shape
    dff = w1.shape[1]
    row = lambda i, j: (i, 0)
    const = lambda i, j: (0, 0)
    return pl.pallas_call(
        _ffn_kernel,
        grid=(n // tm, dff // tf),
        in_specs=[pl.BlockSpec((tm, d), row), pl.BlockSpec((tm, d), row), pl.BlockSpec((d, d), const),
                  pl.BlockSpec((1, d), const), pl.BlockSpec((d, tf), lambda i, j: (0, j)),
                  pl.BlockSpec((d, tf), lambda i, j: (0, j)), pl.BlockSpec((tf, d), lambda i, j: (j, 0))],
        out_specs=pl.BlockSpec((tm, d), row),
        out_shape=jax.ShapeDtypeStruct((n, d), F32),
        scratch_shapes=[pltpu.VMEM((tm, d), BF16)],
        compiler_params=_params("parallel", "arbitrary"),
        name="attn_out_ffn",
    )(x, o, w_o.astype(BF16), norm_g[None, :], w1.astype(BF16), w3.astype(BF16), w2.astype(BF16))


def _gdn_proj_kernel(x_ref, g_ref, w_ref, cw_ref, hist_ref, alog_ref, dtb_ref,
                     q_ref, k_ref, v_ref, gate_ref, aux_ref, hout_ref, pad_s, *, steps_per_seq):
    i = pl.program_id(0)
    tm, d = x_ref.shape
    cdim = 3 * d
    first = (i % steps_per_seq) == 0

    @pl.when(first)
    def _():
        pad_s[0:8, :] = hist_ref[0]

    @pl.when(jnp.logical_not(first))
    def _():
        pad_s[0:8, :] = pad_s[tm:tm + 8, :]

    h = _rms(x_ref[...], g_ref[...])
    y = jnp.dot(h.astype(BF16), w_ref[...], preferred_element_type=F32)
    pad_s[8:8 + tm, :] = y[:, :cdim]
    hout_ref[0] = pad_s[tm:tm + 8, :]

    for blk in range(cdim // LANES):
        cs = slice(blk * LANES, (blk + 1) * LANES)
        mixed = jnp.zeros((tm, LANES), F32)
        for w in range(GDN_CONV):
            mixed = mixed + cw_ref[w:w + 1, cs] * pad_s[8 - (GDN_CONV - 1) + w:8 - (GDN_CONV - 1) + w + tm, cs]
        mixed = _silu(mixed)
        part, hd = divmod(blk, GDN_HEADS)
        hs = slice(hd * LANES, (hd + 1) * LANES)
        if part == 2:
            v_ref[:, hs] = mixed
        else:
            nrm = mixed * lax.rsqrt(jnp.sum(mixed * mixed, axis=1, keepdims=True) + EPS)
            if part == 0:
                q_ref[:, hs] = nrm * (GDN_DH ** -0.5)
            else:
                k_ref[:, hs] = nrm

    gate_ref[...] = _silu(y[:, cdim:cdim + d])
    z = y[:, cdim + d:cdim + d + LANES]
    lane = lax.broadcasted_iota(jnp.int32, z.shape, 1)
    beta = 1.0 / (1.0 + jnp.exp(-z))
    gdec = -jnp.exp(alog_ref[...]) * _softplus(z + dtb_ref[...])
    aux_ref[...] = jnp.where(lane < GDN_HEADS, beta, gdec)


def _gdn_proj(x, hist, b, t, tm, norm_g, w_qkv, conv_w, w_beta, w_a, a_log, dt_bias, w_gate):
    n, d = x.shape
    cdim = 3 * d
    w_all = jnp.concatenate(
        [w_qkv, w_gate, w_beta, w_a, jnp.zeros((d, LANES - 2 * GDN_HEADS), F32)], axis=1).astype(BF16)
    hist8 = jnp.concatenate([jnp.zeros((b, 8 - (GDN_CONV - 1), cdim), F32), hist], axis=1)
    lane_vec = lambda a: jnp.concatenate(
        [jnp.zeros((GDN_HEADS,), F32), a, jnp.zeros((LANES - 2 * GDN_HEADS,), F32)])[None, :]
    steps = t // tm
    row = lambda i: (i, 0)
    const = lambda i: (0, 0)
    outs = pl.pallas_call(
        functools.partial(_gdn_proj_kernel, steps_per_seq=steps),
        grid=(n // tm,),
        in_specs=[pl.BlockSpec((tm, d), row), pl.BlockSpec((1, d), const),
                  pl.BlockSpec((d, 4 * d + LANES), const), pl.BlockSpec((GDN_CONV, cdim), const),
                  pl.BlockSpec((1, 8, cdim), lambda i: (i // steps, 0, 0)),
                  pl.BlockSpec((1, LANES), const), pl.BlockSpec((1, LANES), const)],
        out_specs=[pl.BlockSpec((tm, d), row), pl.BlockSpec((tm, d), row), pl.BlockSpec((tm, d), row),
                   pl.BlockSpec((tm, d), row), pl.BlockSpec((tm, LANES), row),
                   pl.BlockSpec((1, 8, cdim), lambda i: (i // steps, 0, 0))],
        out_shape=[jax.ShapeDtypeStruct((n, d), F32)] * 4 + [jax.ShapeDtypeStruct((n, LANES), F32),
                                                           jax.ShapeDtypeStruct((b, 8, cdim), F32)],
        scratch_shapes=[pltpu.VMEM((tm + 8, cdim), F32)],
        compiler_params=_params("arbitrary"),
        name="gdn_proj",
    )(x, norm_g[None, :], w_all, conv_w, hist8, lane_vec(a_log), lane_vec(dt_bias))
    q, k, v, gate, aux, hout = outs
    return q, k, v, gate, aux, hout[:, 8 - (GDN_CONV - 1):, :]


def _gdn_chunk_kernel(q_ref, k_ref, v_ref, gate_ref, aux_ref, s0_ref, og_ref, o_ref, s_ref, *, chunk):
    ci = pl.program_id(1)
    nb = q_ref.shape[0]
    per = LANES // chunk
    ngroups = GDN_HEADS // per
    nsq = chunk.bit_length() - 2

    @pl.when(ci == 0)
    def _():
        s_ref[...] = s0_ref[...]

    bi = lax.broadcasted_iota(jnp.int32, (LANES, LANES), 0)
    bj = lax.broadcasted_iota(jnp.int32, (LANES, LANES), 1)
    same = (bi // chunk) == (bj // chunk)
    causal = same & (bi >= bj)
    strict = same & (bi > bj)
    eye = (bi == bj).astype(F32)
    tcol = bj // chunk
    nt = (((1,), (1,)), ((), ()))

    chains = []
    for bb in range(nb):
        aux = aux_ref[bb]
        trow = lax.broadcasted_iota(jnp.int32, aux.shape, 0)
        gc = aux
        sh = 1
        while sh < chunk:
            gc = gc + jnp.where(trow >= sh, pltpu.roll(gc, sh, axis=0), 0.0)
            sh *= 2
        glast = gc[chunk - 1:chunk, :]
        for grp in range(ngroups):
            heads = [grp * per + a for a in range(per)]
            stack = lambda ref: jnp.concatenate(
                [ref[bb, :, hd * LANES:(hd + 1) * LANES] for hd in heads], axis=0)
            col = lambda arr, off: jnp.concatenate([arr[:, off + hd:off + hd + 1] for hd in heads], axis=0)
            c = dict(bb=bb, heads=heads, glast=glast)
            c["q2"], c["k2"], c["v2"] = stack(q_ref), stack(k_ref), stack(v_ref)
            c["beta"] = col(aux, 0)
            gcol = col(gc, GDN_HEADS)
            gl = jnp.concatenate([jnp.broadcast_to(glast[:, GDN_HEADS + hd:GDN_HEADS + hd + 1], (chunk, 1))
                                  for hd in heads], axis=0)
            grow = jnp.broadcast_to(gcol, (LANES, LANES)).T
            c["decay"] = jnp.exp(jnp.where(causal, gcol - grow, -jnp.inf))
            c["egc"] = jnp.exp(gcol)
            c["kdec"] = jnp.exp(gl - gcol)
            chains.append(c)

    for c in chains:
        k2b = c["k2"].astype(BF16)
        c["both"] = lax.dot_general(jnp.concatenate([k2b, c["q2"].astype(BF16)], axis=0), k2b, nt,
                                    preferred_element_type=F32)
    for c in chains:
        c["x"] = -jnp.where(strict, c["beta"] * c["both"][:LANES] * c["decay"], 0.0)
        c["attn"] = (c["both"][LANES:] * c["decay"]).astype(BF16)
        xb = c["x"].astype(BF16)
        c["p"] = jnp.dot(xb, xb, preferred_element_type=F32)
        c["tser"] = eye + c["x"]
    for it in range(nsq):
        for c in chains:
            c["pb"] = c["p"].astype(BF16)
            c["tser"] = c["tser"] + jnp.dot(c["tser"].astype(BF16), c["pb"], preferred_element_type=F32)
        if it + 1 < nsq:
            for c in chains:
                c["p"] = jnp.dot(c["pb"], c["pb"], preferred_element_type=F32)
    for c in chains:
        rhs = jnp.concatenate([c["v2"] * c["beta"], c["k2"] * (c["beta"] * c["egc"])], axis=1)
        c["sol"] = rhs + jnp.dot((c["tser"] - eye).astype(BF16), rhs.astype(BF16), preferred_element_type=F32)
    for c in chains:
        w2b = c["sol"][:, LANES:].astype(BF16)
        qd = (c["q2"] * c["egc"]).astype(BF16)
        c["wq"] = []
        for a, hd in enumerate(c["heads"]):
            rs = slice(a * chunk, (a + 1) * chunk)
            c["wq"].append(jnp.dot(jnp.concatenate([w2b[rs], qd[rs]], axis=0), s_ref[c["bb"], hd].astype(BF16),
                                   preferred_element_type=F32))
    for c in chains:
        ws = jnp.concatenate([wq[:chunk] for wq in c["wq"]], axis=0)
        c["qs"] = jnp.concatenate([wq[chunk:] for wq in c["wq"]], axis=0)
        c["vnb"] = (c["sol"][:, :LANES] - ws).astype(BF16)
        c["kdt"] = (c["k2"] * c["kdec"]).T
    for c in chains:
        c["o2"] = c["qs"] + jnp.dot(c["attn"], c["vnb"], preferred_element_type=F32)
        for a, hd in enumerate(c["heads"]):
            kdt_h = jnp.where(tcol == a, c["kdt"], 0.0).astype(BF16)
            decay_s = jnp.exp(c["glast"][:, GDN_HEADS + hd:GDN_HEADS + hd + 1])
            s_ref[c["bb"], hd] = (s_ref[c["bb"], hd] * decay_s
                                  + jnp.dot(kdt_h, c["vnb"], preferred_element_type=F32))
    for c in chains:
        for a, hd in enumerate(c["heads"]):
            rs = slice(a * chunk, (a + 1) * chunk)
            oh = _rms(c["o2"][rs], og_ref[...]) * gate_ref[c["bb"], :, hd * LANES:(hd + 1) * LANES]
            o_ref[c["bb"], :, hd * LANES:(hd + 1) * LANES] = oh.astype(o_ref.dtype)


def _gdn_chunks(q, k, v, gate, aux, s0, out_gain, b, t, chunk, nb):
    n, d = q.shape
    nc = t // chunk
    seq = lambda a: a.reshape(b, t, a.shape[1])
    row = lambda bi, ci: (bi, ci, 0)
    st = lambda bi, ci: (bi, 0, 0, 0)
    sshape = (nb, GDN_HEADS, GDN_DH, GDN_DH)
    o, s_new = pl.pallas_call(
        functools.partial(_gdn_chunk_kernel, chunk=chunk),
        grid=(b // nb, nc),
        in_specs=[pl.BlockSpec((nb, chunk, d), row)] * 4 + [pl.BlockSpec((nb, chunk, LANES), row),
                                                           pl.BlockSpec(sshape, st),
                                                           pl.BlockSpec((1, GDN_DH), lambda bi, ci: (0, 0))],
        out_specs=[pl.BlockSpec((nb, chunk, d), row), pl.BlockSpec(sshape, st)],
        out_shape=[jax.ShapeDtypeStruct((b, t, d), BF16), jax.ShapeDtypeStruct(s0.shape, F32)],
        compiler_params=_params("parallel", "arbitrary"),
        name="gdn_chunks",
    )(seq(q), seq(k), seq(v), seq(gate), seq(aux), s0, out_gain[None, :])
    return o.reshape(n, d), s_new


def _router_kernel(x_ref, o_ref, wo_ref, g_ref, r_ref, base_ref, x1_ref, h3_ref, info_ref, cnt_ref, cnt_s,
                   *, cap):
    i = pl.program_id(0)
    tm, d = x_ref.shape

    @pl.when(i == 0)
    def _():
        cnt_s[...] = base_ref[...]

    x1 = x_ref[...] + jnp.dot(o_ref[...], wo_ref[...], preferred_element_type=F32)
    x1_ref[...] = x1
    h = _rms(x1, g_ref[...])
    nchunk = d // LANES
    for c in range(nchunk):
        h3_ref[pl.ds(c, tm, stride=nchunk), :] = h[:, c * LANES:(c + 1) * LANES]
    logits = jnp.dot(h, r_ref[...], preferred_element_type=F32, precision=HIGHEST)
    lane = lax.broadcasted_iota(jnp.int32, logits.shape, 1)
    logits = jnp.where(lane < N_EXPERTS, logits, -jnp.inf)
    m1 = jnp.max(logits, axis=1, keepdims=True)
    i1 = jnp.min(jnp.where(logits == m1, lane, LANES), axis=1, keepdims=True)
    rest = jnp.where(lane == i1, -jnp.inf, logits)
    m2 = jnp.max(rest, axis=1, keepdims=True)
    i2 = jnp.min(jnp.where(rest == m2, lane, LANES), axis=1, keepdims=True)
    e2 = jnp.exp(m2 - m1)
    g1 = 1.0 / (1.0 + e2)
    g2 = e2 / (1.0 + e2)
    member = ((lane == i1) | (lane == i2)).astype(F32)
    ri = lax.broadcasted_iota(jnp.int32, (tm, tm), 0)
    rj = lax.broadcasted_iota(jnp.int32, (tm, tm), 1)
    rank = jnp.dot((rj < ri).astype(BF16), member.astype(BF16), preferred_element_type=F32)
    slot = cnt_s[...] + rank + lane.astype(F32) * float(cap)
    d1 = jnp.sum(jnp.where(lane == i1, slot, 0.0), axis=1, keepdims=True)
    d2 = jnp.sum(jnp.where(lane == i2, slot, 0.0), axis=1, keepdims=True)
    cnt_s[...] += jnp.sum(member, axis=0, keepdims=True)
    cnt_ref[...] = cnt_s[...]
    info = jnp.zeros(logits.shape, F32)
    for k, val in enumerate((i1.astype(F32), i2.astype(F32), g1, g2, d1, d2)):
        info = jnp.where(lane == k, val, info)
    info_ref[...] = info


def _moe_router(x, o, w_o, norm_g, router, base_counts, tm, cap):
    n, d = x.shape
    r_pad = jnp.concatenate([router, jnp.zeros((d, LANES - N_EXPERTS), F32)], axis=1)
    row = lambda i: (i, 0)
    const = lambda i: (0, 0)
    x1, h3, info, counts = pl.pallas_call(
        functools.partial(_router_kernel, cap=cap),
        grid=(n // tm,),
        in_specs=[pl.BlockSpec((tm, d), row), pl.BlockSpec((tm, d), row), pl.BlockSpec((d, d), const),
                  pl.BlockSpec((1, d), const), pl.BlockSpec((d, LANES), const), pl.BlockSpec((1, LANES), const)],
        out_specs=[pl.BlockSpec((tm, d), row), pl.BlockSpec((tm * (d // LANES), LANES), row),
                   pl.BlockSpec((tm, LANES), row), pl.BlockSpec((1, LANES), const)],
        out_shape=[jax.ShapeDtypeStruct((n, d), F32), jax.ShapeDtypeStruct((n * (d // LANES), LANES), F32),
                   jax.ShapeDtypeStruct((n, LANES), F32), jax.ShapeDtypeStruct((1, LANES), F32)],
        scratch_shapes=[pltpu.VMEM((1, LANES), F32)],
        compiler_params=_params("arbitrary"),
        name="moe_router",
    )(x, o, w_o.astype(BF16), norm_g[None, :], r_pad, base_counts)
    return x1, h3.reshape(n, d // LANES, LANES), info, counts


def _row_dests(info, tm):
    n = info.shape[0]
    dest = info[:, 4:6].astype(jnp.int32)
    return dest.reshape(n // tm, tm, 2).transpose(0, 2, 1).reshape(n // tm, 1, 2 * tm)


def _dispatch_kernel(*refs, tm, final, tile):
    if final:
        dest_ref, pad_ref, h3_ref, _, xs_ref, sem, zeros_s = refs
    else:
        dest_ref, h3_ref, xs_ref, sem = refs
    i = pl.program_id(0)
    nchunk = xs_ref.shape[1]

    def copies(r):
        src = h3_ref.at[pl.ds(pl.multiple_of(r * nchunk, nchunk), nchunk)]
        return (pltpu.make_async_copy(src, xs_ref.at[dest_ref[0, 0, r]], sem),
                pltpu.make_async_copy(src, xs_ref.at[dest_ref[0, 0, tm + r]], sem))

    def issue(r, carry):
        for prio, cp in enumerate(copies(r)):
            cp.start(priority=prio)
        return carry

    lax.fori_loop(0, tm, issue, 0, unroll=8)
    landed = xs_ref.at[pl.ds(0, 2 * tm)]
    pltpu.make_async_copy(landed, landed, sem).wait()

    if final:
        @pl.when(i == pl.num_programs(0) - 1)
        def _():
            zeros_s[...] = jnp.zeros(zeros_s.shape, F32)
            for e in range(N_EXPERTS):
                start = pad_ref[0, e]
                tail = pad_ref[1, e]
                for bit in reversed(range((tile - 1).bit_length())):
                    size = 1 << bit
                    off = start + ((tail >> (bit + 1)) << (bit + 1))

                    @pl.when(((tail >> bit) & 1) == 1)
                    def _():
                        cp = pltpu.make_async_copy(zeros_s.at[pl.ds(0, size)], xs_ref.at[pl.ds(off, size)], sem)
                        cp.start()
                        cp.wait()


def _moe_dispatch(h3, dest, tm, rows, xs=None, pad=None, tile=None):
    n, s, _ = h3.shape
    final = xs is not None
    smem = functools.partial(pl.BlockSpec, memory_space=pltpu.SMEM)
    anyspec = pl.BlockSpec(memory_space=pl.ANY)
    in_specs = [smem((1, 1, 2 * tm), lambda i: (i, 0, 0))]
    args = [dest]
    if final:
        in_specs.append(smem((2, N_EXPERTS), lambda i: (0, 0)))
        args.append(pad)
    in_specs.append(pl.BlockSpec((tm * s, LANES), lambda i: (i, 0)))
    args.append(h3.reshape(n * s, LANES))
    scratch = [pltpu.SemaphoreType.DMA(())]
    if final:
        in_specs.append(anyspec)
        args.append(xs)
        scratch.append(pltpu.VMEM((1 << ((tile - 1).bit_length() - 1), s, LANES), F32))
    return pl.pallas_call(
        functools.partial(_dispatch_kernel, tm=tm, final=final, tile=tile),
        grid=(n // tm,),
        in_specs=in_specs,
        out_specs=anyspec,
        out_shape=jax.ShapeDtypeStruct((rows, s, LANES), F32),
        scratch_shapes=scratch,
        input_output_aliases={len(args) - 1: 0} if final else {},
        compiler_params=_params("arbitrary"),
        name="moe_dispatch",
    )(*args)


def _experts_kernel(blk_ref, exp_ref, valid_ref, xs_ref, w1_ref, w3_ref, w2_ref, ys_ref, h_s, acc_s):
    t = pl.program_id(0)
    j = pl.program_id(1)
    tile, d = h_s.shape
    nchunk = d // LANES

    @pl.when(valid_ref[t] == 1)
    def _():
        @pl.when(j == 0)
        def _():
            for c in range(nchunk):
                h_s[:, c * LANES:(c + 1) * LANES] = xs_ref[pl.ds(c, tile, stride=nchunk), :].astype(BF16)

        part = _swiglu_step(h_s[...], w1_ref[0], w3_ref[0], w2_ref[0])

        @pl.when(j == 0)
        def _():
            acc_s[...] = part

        @pl.when(j > 0)
        def _():
            acc_s[...] += part

        @pl.when(j == pl.num_programs(1) - 1)
        def _():
            for c in range(nchunk):
                ys_ref[pl.ds(c, tile, stride=nchunk), :] = acc_s[:, c * LANES:(c + 1) * LANES]


def _moe_experts(xs, counts, w1, w3, w2, cap, tile, tf, n_assign):
    rows, s, _ = xs.shape
    ne, d, dff = w1.shape
    nj = dff // tf
    n_tiles = -(-n_assign // tile) + ne
    tiles_e = (counts + tile - 1) // tile
    cum = jnp.cumsum(tiles_e)
    total = cum[-1]
    t = jnp.arange(n_tiles, dtype=jnp.int32)
    t_eff = jnp.maximum(jnp.minimum(t, total - 1), 0)
    e_of_t = jnp.minimum(jnp.sum(t_eff[:, None] >= cum[None, :], axis=1), ne - 1).astype(jnp.int32)
    blk = (e_of_t * (cap // tile) + t_eff - (cum - tiles_e)[e_of_t]).astype(jnp.int32)
    valid = (t < total).astype(jnp.int32)
    jmap = lambda j, v: j * v + (nj - 1) * (1 - v)
    grid_spec = pltpu.PrefetchScalarGridSpec(
        num_scalar_prefetch=3,
        grid=(n_tiles, nj),
        in_specs=[pl.BlockSpec((tile * s, LANES), lambda t, j, blk, ex, va: (blk[t], 0)),
                  pl.BlockSpec((1, d, tf), lambda t, j, blk, ex, va: (ex[t], 0, jmap(j, va[t]))),
                  pl.BlockSpec((1, d, tf), lambda t, j, blk, ex, va: (ex[t], 0, jmap(j, va[t]))),
                  pl.BlockSpec((1, tf, d), lambda t, j, blk, ex, va: (ex[t], jmap(j, va[t]), 0))],
        out_specs=pl.BlockSpec((tile * s, LANES), lambda t, j, blk, ex, va: (blk[t], 0)),
        scratch_shapes=[pltpu.VMEM((tile, d), BF16), pltpu.VMEM((tile, d), F32)])
    ys = pl.pallas_call(
        _experts_kernel,
        grid_spec=grid_spec,
        out_shape=jax.ShapeDtypeStruct((rows * s, LANES), F32),
        compiler_params=_params("arbitrary", "arbitrary"),
        name="moe_experts",
    )(blk, e_of_t, valid, xs.reshape(rows * s, LANES), w1, w3, w2)
    return ys.reshape(rows, s, LANES)


def _combine_kernel(dest_ref, x1_ref, info_ref, ys_ref, y_ref, buf, sem):
    tm, d = x1_ref.shape
    nchunk = d // LANES

    def slab(k):
        return buf.at[pl.ds(pl.multiple_of(k * nchunk, nchunk), nchunk)]

    def copies(r):
        return (pltpu.make_async_copy(ys_ref.at[dest_ref[0, 0, r]], slab(r), sem),
                pltpu.make_async_copy(ys_ref.at[dest_ref[0, 0, tm + r]], slab(tm + r), sem))

    def issue(r, carry):
        for prio, cp in enumerate(copies(r)):
            cp.start(priority=prio)
        return carry

    lax.fori_loop(0, tm, issue, 0, unroll=8)
    pltpu.make_async_copy(buf, buf, sem).wait()
    g1 = info_ref[:, 2:3]
    g2 = info_ref[:, 3:4]
    for c in range(nchunk):
        cs = slice(c * LANES, (c + 1) * LANES)
        r1 = buf[pl.ds(c, tm, stride=nchunk), :]
        r2 = buf[pl.ds(tm * nchunk + c, tm, stride=nchunk), :]
        y_ref[:, cs] = x1_ref[:, cs] + g1 * r1 + g2 * r2


def _moe_combine(x1, info, dest, ys, tm):
    n, d = x1.shape
    s = d // LANES
    row = lambda i: (i, 0)
    return pl.pallas_call(
        _combine_kernel,
        grid=(n // tm,),
        in_specs=[pl.BlockSpec((1, 1, 2 * tm), lambda i: (i, 0, 0), memory_space=pltpu.SMEM),
                  pl.BlockSpec((tm, d), row), pl.BlockSpec((tm, LANES), row),
                  pl.BlockSpec(memory_space=pl.ANY)],
        out_specs=pl.BlockSpec((tm, d), row),
        out_shape=jax.ShapeDtypeStruct((n, d), F32),
        scratch_shapes=[pltpu.VMEM((2 * tm * s, LANES), F32), pltpu.SemaphoreType.DMA(())],
        compiler_params=_params("arbitrary"),
        name="moe_combine",
    )(dest, x1, info, ys)


def _moe_block(streams, moe_w, w_o):
    moe_norm, router, w1, w3, w2 = moe_w
    n_total = sum(x.shape[0] for x, _ in streams)
    share = -(-2 * n_total * 11 // (10 * N_EXPERTS))
    tile = -(-share // (MOE_TILES_PER_EXPERT * LANES)) * LANES
    cap = -(-n_total // tile) * tile
    rows = N_EXPERTS * cap
    counts = jnp.zeros((1, LANES), F32)
    routed = []
    for x, o in streams:
        tm = min(512, x.shape[0])
        x1, h3, info, counts = _moe_router(x, o, w_o, moe_norm, router, counts, tm, cap)
        routed.append((x1, h3, info, _row_dests(info, tm), tm))
    cnt = counts[0, :N_EXPERTS].astype(jnp.int32)
    pad = jnp.stack([jnp.arange(N_EXPERTS, dtype=jnp.int32) * cap + cnt, (-cnt) % tile])
    xs = None
    for k, (x1, h3, info, dest, tm) in enumerate(routed):
        if k + 1 < len(routed):
            assert xs is None, "only the first and the final dispatch are distinguished"
            xs = _moe_dispatch(h3, dest, tm, rows)
        else:
            xs = _moe_dispatch(h3, dest, tm, rows, xs=xs, pad=pad, tile=tile)
    ys = _moe_experts(xs, cnt, w1, w3, w2, cap, tile, MOE_TF, 2 * n_total)
    return [_moe_combine(x1, info, dest, ys, tm) for x1, h3, info, dest, tm in routed]


def _fox_layer(x, b, t, cache, w, ffn_w):
    n, d = x.shape
    tm = min(512, n)
    q, k, v, logf = _fox_proj(x, *w[:6], tm=tm)
    if cache is None:
        c = _cumsum_time(logf.reshape(b, t, FOX_HEADS))
        o = _fox_attn_prompt(q, k, v, c, b, t, tq=min(512, t), tk=min(512, t))
    else:
        ck, cv, cl = cache
        past = ck.shape[1]
        c = _cumsum_time(jnp.concatenate([cl, logf.reshape(b, t, FOX_HEADS)], axis=1))
        o = _fox_attn_sample(q, k, v, ck.reshape(b, past, d), cv.reshape(b, past, d), c, b, t)
    y = _attn_out_ffn(x, o, w[6], *ffn_w, tm=min(1024, n), tf=512)
    return y, k, v, logf


def _gdn_mixer(x, b, t, chunk, hist, s0, w):
    norm_g, w_qkv, conv_w, w_beta, w_a, a_log, dt_bias, w_gate, out_gain, _ = w
    tm = min(512, t)
    q, k, v, gate, aux, new_hist = _gdn_proj(x, hist, b, t, tm, norm_g, w_qkv, conv_w, w_beta, w_a,
                                             a_log, dt_bias, w_gate)
    groups_per_seq = GDN_HEADS // (LANES // chunk)
    nb = max(1, min(b, GDN_CHAINS_PER_STEP // groups_per_seq))
    o, s_new = _gdn_chunks(q, k, v, gate, aux, s0, out_gain, b, t, chunk, nb)
    return o, s_new, new_hist


def kernel(x_prompt, x_sample, cache_fox_k, cache_fox_v, cache_fox_logf, state_gdn_s, state_gdn_conv, fox_norm, fox_w_qkv, fox_q_gain, fox_k_gain, fox_w_f, fox_b_f, fox_w_o, ffn_norm, ffn_w1, ffn_w3, ffn_w2, gdn_norm, gdn_w_qkv, gdn_conv_w, gdn_w_beta, gdn_w_a, gdn_a_log, gdn_dt_bias, gdn_w_gate, gdn_out_gain, gdn_w_o, moe_norm, moe_router, moe_w1, moe_w3, moe_w2):
    bp, tp, d = x_prompt.shape
    bs, ts, _ = x_sample.shape
    depth = fox_norm.shape[0] + gdn_norm.shape[0]
    cdim = gdn_w_qkv.shape[2]
    yp = x_prompt.reshape(bp * tp, d)
    ys = x_sample.reshape(bs * ts, d)
    kp_l, vp_l, lp_l, ks_l, vs_l, ls_l = [], [], [], [], [], []
    sp_l, cp_l, ss_l, cs_l = [], [], [], []
    for i in range(depth):
        j = i // 2
        if i % 2 == 0:
            fw = (fox_norm[j], fox_w_qkv[j], fox_q_gain[j], fox_k_gain[j], fox_w_f[j], fox_b_f[j], fox_w_o[j])
            ffw = (ffn_norm[j], ffn_w1[j], ffn_w3[j], ffn_w2[j])
            yp, kp, vp, lp = _fox_layer(yp, bp, tp, None, fw, ffw)
            ys, ks, vs, ls = _fox_layer(ys, bs, ts, (cache_fox_k[j], cache_fox_v[j], cache_fox_logf[j]), fw, ffw)
            kp_l.append(kp.reshape(bp, tp, FOX_HEADS, FOX_DH)); vp_l.append(vp.reshape(bp, tp, FOX_HEADS, FOX_DH))
            lp_l.append(lp.reshape(bp, tp, FOX_HEADS))
            ks_l.append(ks.reshape(bs, ts, FOX_HEADS, FOX_DH)); vs_l.append(vs.reshape(bs, ts, FOX_HEADS, FOX_DH))
            ls_l.append(ls.reshape(bs, ts, FOX_HEADS))
        else:
            gw = (gdn_norm[j], gdn_w_qkv[j], gdn_conv_w[j], gdn_w_beta[j], gdn_w_a[j], gdn_a_log[j],
                  gdn_dt_bias[j], gdn_w_gate[j], gdn_out_gain[j], gdn_w_o[j])
            mw = (moe_norm[j], moe_router[j], moe_w1[j], moe_w3[j], moe_w2[j])
            hist0 = jnp.zeros((bp, GDN_CONV - 1, cdim), F32)
            s_zero = jnp.zeros((bp, GDN_HEADS, GDN_DH, GDN_DH), F32)
            op, sp, cp = _gdn_mixer(yp, bp, tp, 64, hist0, s_zero, gw)
            os_, ss, cs = _gdn_mixer(ys, bs, ts, ts, state_gdn_conv[j], state_gdn_s[j], gw)
            yp, ys = _moe_block([(yp, op), (ys, os_)], mw, gw[-1])
            sp_l.append(sp); cp_l.append(cp); ss_l.append(ss); cs_l.append(cs)
    return (yp.reshape(bp, tp, d), ys.reshape(bs, ts, d),
            jnp.stack(kp_l), jnp.stack(vp_l), jnp.stack(lp_l),
            jnp.stack(ks_l), jnp.stack(vs_l), jnp.stack(ls_l),
            jnp.stack(sp_l), jnp.stack(cp_l), jnp.stack(ss_l), jnp.stack(cs_l))
```

```python
import functools

import jax
import jax.numpy as jnp
from jax import lax
from jax.experimental import pallas as pl
from jax.experimental.pallas import tpu as pltpu

F32 = jnp.float32
BF16 = jnp.bfloat16
EPS = 1e-6
LANES = 128
VMEM_LIMIT = 56 * 1024 * 1024

FOX_HEADS = 16
FOX_DH = 64
GDN_HEADS = 8
GDN_DH = 128
GDN_CONV = 4
N_EXPERTS = 8
MOE_TILES_PER_EXPERT = 4
MOE_TF = 512
ROUTER_ROW_GROUPS = 2
GDN_CHAINS_PER_STEP = 16
HIGHEST = lax.Precision.HIGHEST


def _params(*sem):
    return pltpu.CompilerParams(dimension_semantics=sem, vmem_limit_bytes=VMEM_LIMIT)


def _rms(x, gain):
    return x * lax.rsqrt(jnp.mean(x * x, axis=-1, keepdims=True) + EPS) * gain


def _silu(x):
    return x * (1.0 / (1.0 + jnp.exp(-x)))


def _softplus(x):
    return jnp.maximum(x, 0.0) + jnp.log(1.0 + jnp.exp(-jnp.abs(x)))


def _split3(c):
    hi = c.astype(BF16).astype(F32)
    r = c - hi
    mid = r.astype(BF16).astype(F32)
    lo = (r - mid).astype(BF16).astype(F32)
    return hi, mid, lo


def _fox_proj_kernel(x_ref, g_ref, w_ref, qg_ref, kg_ref, bf_ref, gm_ref, gmt_ref,
                     q_ref, k_ref, v_ref, lf_ref):
    d = x_ref.shape[1]
    h = _rms(x_ref[...], g_ref[...])
    y = jnp.dot(h.astype(BF16), w_ref[...], preferred_element_type=F32)

    def headnorm(z, gain):
        ss = jnp.dot((z * z).astype(BF16), gm_ref[...], preferred_element_type=F32)
        inv = lax.rsqrt(ss * (1.0 / FOX_DH) + EPS)
        hi = inv.astype(BF16)
        lo = (inv - hi.astype(F32)).astype(BF16)
        full = (jnp.dot(hi, gmt_ref[...], preferred_element_type=F32)
                + jnp.dot(lo, gmt_ref[...], preferred_element_type=F32))
        return z * full * gain

    q_ref[...] = (headnorm(y[:, :d], qg_ref[...]) * (FOX_DH ** -0.5)).astype(q_ref.dtype)
    k_ref[...] = headnorm(y[:, d:2 * d], kg_ref[...])
    v_ref[...] = y[:, 2 * d:3 * d]
    z = y[:, 3 * d:3 * d + LANES] + bf_ref[...]
    lf = jnp.minimum(z, 0.0) - jnp.log(1.0 + jnp.exp(-jnp.abs(z)))
    lf_ref[...] = lf[:, :FOX_HEADS]


def _fox_proj(x, norm_g, w_qkv, q_gain, k_gain, w_f, b_f, tm):
    n, d = x.shape
    w_all = jnp.concatenate(
        [w_qkv, w_f, jnp.zeros((d, LANES - FOX_HEADS), F32)], axis=1).astype(BF16)
    bf = jnp.concatenate([b_f, jnp.zeros((LANES - FOX_HEADS,), F32)])[None, :]
    head_of_col = jnp.arange(d) // FOX_DH
    gm = (head_of_col[:, None] == jnp.arange(LANES)[None, :]).astype(BF16)
    gmt = gm.T
    qg = jnp.tile(q_gain, FOX_HEADS)[None, :]
    kg = jnp.tile(k_gain, FOX_HEADS)[None, :]
    const = lambda i: (0, 0)
    row = lambda i: (i, 0)
    return pl.pallas_call(
        _fox_proj_kernel,
        grid=(n // tm,),
        in_specs=[pl.BlockSpec((tm, d), row), pl.BlockSpec((1, d), const),
                  pl.BlockSpec((d, 3 * d + LANES), const), pl.BlockSpec((1, d), const),
                  pl.BlockSpec((1, d), const), pl.BlockSpec((1, LANES), const),
                  pl.BlockSpec((d, LANES), const), pl.BlockSpec((LANES, d), const)],
        out_specs=[pl.BlockSpec((tm, d), row), pl.BlockSpec((tm, d), row), pl.BlockSpec((tm, d), row),
                   pl.BlockSpec((tm, FOX_HEADS), row)],
        out_shape=[jax.ShapeDtypeStruct((n, d), BF16), jax.ShapeDtypeStruct((n, d), F32),
                   jax.ShapeDtypeStruct((n, d), F32), jax.ShapeDtypeStruct((n, FOX_HEADS), F32)],
        compiler_params=_params("parallel"),
        name="fox_proj",
    )(x, norm_g[None, :], w_all, qg, kg, bf, gm, gmt)


def _cumsum_kernel(x_ref, o_ref):
    x = x_ref[0]
    t = x.shape[0]
    row = lax.broadcasted_iota(jnp.int32, x.shape, 0)
    s = 1
    while s < t:
        x = x + jnp.where(row >= s, pltpu.roll(x, s, axis=0), 0.0)
        s *= 2
    o_ref[0] = x


def _cumsum_time(x):
    b, t, hn = x.shape
    spec = pl.BlockSpec((1, t, hn), lambda i: (i, 0, 0))
    return pl.pallas_call(
        _cumsum_kernel, grid=(b,), in_specs=[spec], out_specs=spec,
        out_shape=jax.ShapeDtypeStruct(x.shape, F32),
        compiler_params=_params("parallel"), name="logf_cumsum",
    )(x)


def _head_col(c16, head):
    lane = lax.broadcasted_iota(jnp.int32, c16.shape, 1)
    return jnp.sum(jnp.where(lane == head, c16, 0.0), axis=1, keepdims=True)


def _aug(data, ccol, hh, is_q):
    lane = lax.broadcasted_iota(jnp.int32, data.shape, 1)
    base = FOX_DH if hh == 0 else 0
    in_data = (lane < FOX_DH) if hh == 0 else (lane >= FOX_DH)
    hi, mid, lo = _split3(ccol)
    lane1 = lax.broadcasted_iota(jnp.int32, (1, data.shape[1]), 1)
    if is_q:
        pieces = {0: hi, 1: mid, 2: lo}
        ones = (lane1 >= base + 3) & (lane1 < base + 6)
    else:
        pieces = {3: -hi, 4: -mid, 5: -lo}
        ones = (lane1 >= base) & (lane1 < base + 3)
    aug = jnp.where(ones, 1.0, 0.0)
    for i, p in pieces.items():
        aug = jnp.where(lane == base + i, p, aug)
    return jnp.where(in_data, data, aug)


def _v_aug(v2, hh):
    lane = lax.broadcasted_iota(jnp.int32, v2.shape, 1)
    in_data = (lane < FOX_DH) if hh == 0 else (lane >= FOX_DH)
    return jnp.where(in_data, v2, 1.0)


def _finish_heads(acc0, acc1):
    lane = lax.broadcasted_iota(jnp.int32, acc0.shape, 1)
    o0 = acc0 / pltpu.roll(acc0, FOX_DH, axis=1)
    o1 = acc1 / pltpu.roll(acc1, FOX_DH, axis=1)
    return jnp.where(lane < FOX_DH, o0, o1)


def _fox_attn_kernel(q_ref, k_ref, v_ref, c_ref, o_ref, kt_s, v_s, *, tq, tk):
    hp = pl.program_id(1)
    t = k_ref.shape[0]
    nkb = t // tk
    nd = tq // tk
    c16 = c_ref[0]
    k2 = k_ref[...]
    v2 = v_ref[...]
    ccols = [_head_col(c16, 2 * hp + hh) for hh in range(2)]
    for hh in range(2):
        kaug = _aug(k2, ccols[hh], hh, False)
        for jb in range(nkb):
            kt_s[hh, jb] = kaug[jb * tk:(jb + 1) * tk, :].T.astype(BF16)
        v_s[hh] = _v_aug(v2, hh).astype(BF16)

    rows = lax.broadcasted_iota(jnp.int32, (tq, tk), 0)
    cols = lax.broadcasted_iota(jnp.int32, (tq, tk), 1)
    chains = []
    for i in range(t // tq):
        q2 = q_ref[i * tq:(i + 1) * tq, :].astype(F32)
        for hh in range(2):
            qb = _aug(q2, ccols[hh][i * tq:(i + 1) * tq], hh, True).astype(BF16)
            order = list(range(i * nd, (i + 1) * nd)) + list(range(i * nd))
            chains.append(dict(i=i, hh=hh, qb=qb, order=order, m=None, acc=None))
    for rnd in range(max(len(c["order"]) for c in chains)):
        live = [c for c in chains if rnd < len(c["order"])]
        for c in live:
            j = c["order"][rnd]
            s = jnp.dot(c["qb"], kt_s[c["hh"], j], preferred_element_type=F32)
            dblk = j - c["i"] * nd
            if dblk >= 0:
                s = jnp.where(cols + dblk * tk <= rows, s, -jnp.inf)
            c["s"] = s
        for c in live:
            bm = jnp.max(c["s"], axis=1, keepdims=True)
            m_new = bm if c["m"] is None else jnp.maximum(c["m"], bm)
            c["p"] = jnp.exp(c["s"] - m_new).astype(BF16)
            c["alpha"] = None if c["m"] is None else jnp.exp(c["m"] - m_new)
            c["m"] = m_new
        for c in live:
            j = c["order"][rnd]
            pv = jnp.dot(c["p"], v_s[c["hh"], j * tk:(j + 1) * tk, :], preferred_element_type=F32)
            c["acc"] = pv if c["acc"] is None else c["alpha"] * c["acc"] + pv
    for i in range(t // tq):
        o_ref[i * tq:(i + 1) * tq, :] = _finish_heads(chains[2 * i]["acc"],
                                                       chains[2 * i + 1]["acc"]).astype(o_ref.dtype)


def _fox_attn_prompt(q, k, v, c, b, t, tq, tk):
    n, d = q.shape
    nhp = d // LANES
    blk = pl.BlockSpec((t, LANES), lambda bi, hp: (bi, hp))
    return pl.pallas_call(
        functools.partial(_fox_attn_kernel, tq=tq, tk=tk),
        grid=(b, nhp),
        in_specs=[blk, blk, blk, pl.BlockSpec((1, t, FOX_HEADS), lambda bi, hp: (bi, 0, 0))],
        out_specs=blk,
        out_shape=jax.ShapeDtypeStruct((n, d), BF16),
        scratch_shapes=[pltpu.VMEM((2, t // tk, LANES, tk), BF16), pltpu.VMEM((2, t, LANES), BF16)],
        compiler_params=_params("parallel", "parallel"),
        name="fox_attn_prompt",
    )(q, k, v, c)


def _fox_attn_sample_kernel(q_ref, kn_ref, vn_ref, kc_ref, vc_ref, c_ref, o_ref):
    t, d = q_ref.shape
    past = kc_ref.shape[0]
    c16 = c_ref[0]
    rows = lax.broadcasted_iota(jnp.int32, (t, t), 0)
    cols = lax.broadcasted_iota(jnp.int32, (t, t), 1)
    nt = (((1,), (1,)), ((), ()))
    for hp in range(d // LANES):
        ls = slice(hp * LANES, (hp + 1) * LANES)
        q2 = q_ref[:, ls].astype(F32)
        accs = []
        for hh in range(2):
            ccol = _head_col(c16, 2 * hp + hh)
            qaug = _aug(q2, ccol[past:], hh, True).astype(BF16)
            kc = _aug(kc_ref[:, ls], ccol[:past], hh, False).astype(BF16)
            kn = _aug(kn_ref[:, ls], ccol[past:], hh, False).astype(BF16)
            s_c = lax.dot_general(qaug, kc, nt, preferred_element_type=F32)
            s_n = lax.dot_general(qaug, kn, nt, preferred_element_type=F32)
            s_n = jnp.where(cols <= rows, s_n, -jnp.inf)
            m = jnp.maximum(jnp.max(s_c, axis=1, keepdims=True), jnp.max(s_n, axis=1, keepdims=True))
            p_c = jnp.exp(s_c - m).astype(BF16)
            p_n = jnp.exp(s_n - m).astype(BF16)
            acc = (jnp.dot(p_c, _v_aug(vc_ref[:, ls], hh).astype(BF16), preferred_element_type=F32)
                   + jnp.dot(p_n, _v_aug(vn_ref[:, ls], hh).astype(BF16), preferred_element_type=F32))
            accs.append(acc)
        o_ref[:, ls] = _finish_heads(accs[0], accs[1]).astype(o_ref.dtype)


def _fox_attn_sample(q, k_new, v_new, cache_k, cache_v, c_all, b, t):
    n, d = q.shape
    past = cache_k.shape[1]
    new_spec = pl.BlockSpec((t, d), lambda bi: (bi, 0))
    cache_spec = pl.BlockSpec((None, past, d), lambda bi: (bi, 0, 0))
    return pl.pallas_call(
        _fox_attn_sample_kernel,
        grid=(b,),
        in_specs=[new_spec, new_spec, new_spec, cache_spec, cache_spec,
                  pl.BlockSpec((1, past + t, FOX_HEADS), lambda bi: (bi, 0, 0))],
        out_specs=new_spec,
        out_shape=jax.ShapeDtypeStruct((n, d), BF16),
        compiler_params=_params("parallel"),
        name="fox_attn_sample",
    )(q, k_new, v_new, cache_k, cache_v, c_all)


def _swiglu_step(h, w1, w3, w2, nsplit=2):
    tf = w1.shape[1]
    g = tf // nsplit
    ups = []
    for s in range(nsplit):
        cs = slice(s * g, (s + 1) * g)
        ups.append((jnp.dot(h, w1[:, cs].astype(BF16), preferred_element_type=F32),
                    jnp.dot(h, w3[:, cs].astype(BF16), preferred_element_type=F32)))
    out = None
    for s, (a, b) in enumerate(ups):
        t = (_silu(a) * b).astype(BF16)
        part = jnp.dot(t, w2[s * g:(s + 1) * g, :].astype(BF16), preferred_element_type=F32)
        out = part if out is None else out + part
    return out


def _ffn_kernel(x_ref, o_ref, wo_ref, g_ref, w1_ref, w3_ref, w2_ref, y_ref, h_s):
    j = pl.program_id(1)

    @pl.when(j == 0)
    def _():
        x1 = x_ref[...] + jnp.dot(o_ref[...], wo_ref[...], preferred_element_type=F32)
        y_ref[...] = x1
        h_s[...] = _rms(x1, g_ref[...]).astype(BF16)

    y_ref[...] += _swiglu_step(h_s[...], w1_ref[...], w3_ref[...], w2_ref[...])


def _attn_out_ffn(x, o, w_o, norm_g, w1, w3, w2, tm, tf):
    n, d = x.shape
    dff = w1.shape[1]
    row = lambda i, j: (i, 0)
    const = lambda i, j: (0, 0)
    return pl.pallas_call(
        _ffn_kernel,
        grid=(n // tm, dff // tf),
        in_specs=[pl.BlockSpec((tm, d), row), pl.BlockSpec((tm, d), row), pl.BlockSpec((d, d), const),
                  pl.BlockSpec((1, d), const), pl.BlockSpec((d, tf), lambda i, j: (0, j)),
                  pl.BlockSpec((d, tf), lambda i, j: (0, j)), pl.BlockSpec((tf, d), lambda i, j: (j, 0))],
        out_specs=pl.BlockSpec((tm, d), row),
        out_shape=jax.ShapeDtypeStruct((n, d), F32),
        scratch_shapes=[pltpu.VMEM((tm, d), BF16)],
        compiler_params=_params("parallel", "arbitrary"),
        name="attn_out_ffn",
    )(x, o, w_o.astype(BF16), norm_g[None, :], w1.astype(BF16), w3.astype(BF16), w2.astype(BF16))


def _gdn_proj_kernel(x_ref, g_ref, w_ref, cw_ref, hist_ref, alog_ref, dtb_ref,
                     q_ref, k_ref, v_ref, gate_ref, aux_ref, hout_ref, pad_s, *, steps_per_seq):
    i = pl.program_id(0)
    tm, d = x_ref.shape
    cdim = 3 * d
    first = (i % steps_per_seq) == 0

    @pl.when(first)
    def _():
        pad_s[0:8, :] = hist_ref[0]

    @pl.when(jnp.logical_not(first))
    def _():
        pad_s[0:8, :] = pad_s[tm:tm + 8, :]

    h = _rms(x_ref[...], g_ref[...])
    y = jnp.dot(h.astype(BF16), w_ref[...], preferred_element_type=F32)
    pad_s[8:8 + tm, :] = y[:, :cdim]
    hout_ref[0] = pad_s[tm:tm + 8, :]

    for blk in range(cdim // LANES):
        cs = slice(blk * LANES, (blk + 1) * LANES)
        mixed = jnp.zeros((tm, LANES), F32)
        for w in range(GDN_CONV):
            mixed = mixed + cw_ref[w:w + 1, cs] * pad_s[8 - (GDN_CONV - 1) + w:8 - (GDN_CONV - 1) + w + tm, cs]
        mixed = _silu(mixed)
        part, hd = divmod(blk, GDN_HEADS)
        hs = slice(hd * LANES, (hd + 1) * LANES)
        if part == 2:
            v_ref[:, hs] = mixed
        else:
            nrm = mixed * lax.rsqrt(jnp.sum(mixed * mixed, axis=1, keepdims=True) + EPS)
            if part == 0:
                q_ref[:, hs] = nrm * (GDN_DH ** -0.5)
            else:
                k_ref[:, hs] = nrm

    gate_ref[...] = _silu(y[:, cdim:cdim + d])
    z = y[:, cdim + d:cdim + d + LANES]
    lane = lax.broadcasted_iota(jnp.int32, z.shape, 1)
    beta = 1.0 / (1.0 + jnp.exp(-z))
    gdec = -jnp.exp(alog_ref[...]) * _softplus(z + dtb_ref[...])
    aux_ref[...] = jnp.where(lane < GDN_HEADS, beta, gdec)


def _gdn_proj(x, hist, b, t, tm, norm_g, w_qkv, conv_w, w_beta, w_a, a_log, dt_bias, w_gate):
    n, d = x.shape
    cdim = 3 * d
    w_all = jnp.concatenate(
        [w_qkv, w_gate, w_beta, w_a, jnp.zeros((d, LANES - 2 * GDN_HEADS), F32)], axis=1).astype(BF16)
    hist8 = jnp.concatenate([jnp.zeros((b, 8 - (GDN_CONV - 1), cdim), F32), hist], axis=1)
    lane_vec = lambda a: jnp.concatenate(
        [jnp.zeros((GDN_HEADS,), F32), a, jnp.zeros((LANES - 2 * GDN_HEADS,), F32)])[None, :]
    steps = t // tm
    row = lambda i: (i, 0)
    const = lambda i: (0, 0)
    outs = pl.pallas_call(
        functools.partial(_gdn_proj_kernel, steps_per_seq=steps),
        grid=(n // tm,),
        in_specs=[pl.BlockSpec((tm, d), row), pl.BlockSpec((1, d), const),
                  pl.BlockSpec((d, 4 * d + LANES), const), pl.BlockSpec((GDN_CONV, cdim), const),
                  pl.BlockSpec((1, 8, cdim), lambda i: (i // steps, 0, 0)),
                  pl.BlockSpec((1, LANES), const), pl.BlockSpec((1, LANES), const)],
        out_specs=[pl.BlockSpec((tm, d), row), pl.BlockSpec((tm, d), row), pl.BlockSpec((tm, d), row),
                   pl.BlockSpec((tm, d), row), pl.BlockSpec((tm, LANES), row),
                   pl.BlockSpec((1, 8, cdim), lambda i: (i // steps, 0, 0))],
        out_shape=[jax.ShapeDtypeStruct((n, d), F32)] * 4 + [jax.ShapeDtypeStruct((n, LANES), F32),
                                                           jax.ShapeDtypeStruct((b, 8, cdim), F32)],
        scratch_shapes=[pltpu.VMEM((tm + 8, cdim), F32)],
        compiler_params=_params("arbitrary"),
        name="gdn_proj",
    )(x, norm_g[None, :], w_all, conv_w, hist8, lane_vec(a_log), lane_vec(dt_bias))
    q, k, v, gate, aux, hout = outs
    return q, k, v, gate, aux, hout[:, 8 - (GDN_CONV - 1):, :]


def _gdn_chunk_kernel(q_ref, k_ref, v_ref, gate_ref, aux_ref, s0_ref, og_ref, o_ref, s_ref, *, chunk):
    ci = pl.program_id(1)
    nb = q_ref.shape[0]
    per = LANES // chunk
    ngroups = GDN_HEADS // per
    nsq = chunk.bit_length() - 2

    @pl.when(ci == 0)
    def _():
        s_ref[...] = s0_ref[...]

    bi = lax.broadcasted_iota(jnp.int32, (LANES, LANES), 0)
    bj = lax.broadcasted_iota(jnp.int32, (LANES, LANES), 1)
    same = (bi // chunk) == (bj // chunk)
    causal = same & (bi >= bj)
    strict = same & (bi > bj)
    eye = (bi == bj).astype(F32)
    tcol = bj // chunk
    nt = (((1,), (1,)), ((), ()))

    chains = []
    for bb in range(nb):
        aux = aux_ref[bb]
        trow = lax.broadcasted_iota(jnp.int32, aux.shape, 0)
        gc = aux
        sh = 1
        while sh < chunk:
            gc = gc + jnp.where(trow >= sh, pltpu.roll(gc, sh, axis=0), 0.0)
            sh *= 2
        glast = gc[chunk - 1:chunk, :]
        for grp in range(ngroups):
            heads = [grp * per + a for a in range(per)]
            stack = lambda ref: jnp.concatenate(
                [ref[bb, :, hd * LANES:(hd + 1) * LANES] for hd in heads], axis=0)
            col = lambda arr, off: jnp.concatenate([arr[:, off + hd:off + hd + 1] for hd in heads], axis=0)
            c = dict(bb=bb, heads=heads, glast=glast)
            c["q2"], c["k2"], c["v2"] = stack(q_ref), stack(k_ref), stack(v_ref)
            c["beta"] = col(aux, 0)
            gcol = col(gc, GDN_HEADS)
            gl = jnp.concatenate([jnp.broadcast_to(glast[:, GDN_HEADS + hd:GDN_HEADS + hd + 1], (chunk, 1))
                                  for hd in heads], axis=0)
            grow = jnp.broadcast_to(gcol, (LANES, LANES)).T
            c["decay"] = jnp.exp(jnp.where(causal, gcol - grow, -jnp.inf))
            c["egc"] = jnp.exp(gcol)
            c["kdec"] = jnp.exp(gl - gcol)
            chains.append(c)

    for c in chains:
        k2b = c["k2"].astype(BF16)
        c["both"] = lax.dot_general(jnp.concatenate([k2b, c["q2"].astype(BF16)], axis=0), k2b, nt,
                                    preferred_element_type=F32)
    for c in chains:
        c["x"] = -jnp.where(strict, c["beta"] * c["both"][:LANES] * c["decay"], 0.0)
        c["attn"] = (c["both"][LANES:] * c["decay"]).astype(BF16)
        xb = c["x"].astype(BF16)
        c["p"] = jnp.dot(xb, xb, preferred_element_type=F32)
        c["tser"] = eye + c["x"]
    for it in range(nsq):
        for c in chains:
            c["pb"] = c["p"].astype(BF16)
            c["tser"] = c["tser"] + jnp.dot(c["tser"].astype(BF16), c["pb"], preferred_element_type=F32)
        if it + 1 < nsq:
            for c in chains:
                c["p"] = jnp.dot(c["pb"], c["pb"], preferred_element_type=F32)
    for c in chains:
        rhs = jnp.concatenate([c["v2"] * c["beta"], c["k2"] * (c["beta"] * c["egc"])], axis=1)
        c["sol"] = rhs + jnp.dot((c["tser"] - eye).astype(BF16), rhs.astype(BF16), preferred_element_type=F32)
    for c in chains:
        w2b = c["sol"][:, LANES:].astype(BF16)
        qd = (c["q2"] * c["egc"]).astype(BF16)
        c["wq"] = []
        for a, hd in enumerate(c["heads"]):
            rs = slice(a * chunk, (a + 1) * chunk)
            c["wq"].append(jnp.dot(jnp.concatenate([w2b[rs], qd[rs]], axis=0), s_ref[c["bb"], hd].astype(BF16),
                                   preferred_element_type=F32))
    for c in chains:
        ws = jnp.concatenate([wq[:chunk] for wq in c["wq"]], axis=0)
        c["qs"] = jnp.concatenate([wq[chunk:] for wq in c["wq"]], axis=0)
        c["vnb"] = (c["sol"][:, :LANES] - ws).astype(BF16)
        c["kdt"] = (c["k2"] * c["kdec"]).T
    for c in chains:
        c["o2"] = c["qs"] + jnp.dot(c["attn"], c["vnb"], preferred_element_type=F32)
        for a, hd in enumerate(c["heads"]):
            kdt_h = jnp.where(tcol == a, c["kdt"], 0.0).astype(BF16)
            decay_s = jnp.exp(c["glast"][:, GDN_HEADS + hd:GDN_HEADS + hd + 1])
            s_ref[c["bb"], hd] = (s_ref[c["bb"], hd] * decay_s
                                  + jnp.dot(kdt_h, c["vnb"], preferred_element_type=F32))
    for c in chains:
        for a, hd in enumerate(c["heads"]):
            rs = slice(a * chunk, (a + 1) * chunk)
            oh = _rms(c["o2"][rs], og_ref[...]) * gate_ref[c["bb"], :, hd * LANES:(hd + 1) * LANES]
            o_ref[c["bb"], :, hd * LANES:(hd + 1) * LANES] = oh.astype(o_ref.dtype)


def _gdn_chunks(q, k, v, gate, aux, s0, out_gain, b, t, chunk, nb):
    n, d = q.shape
    nc = t // chunk
    seq = lambda a: a.reshape(b, t, a.shape[1])
    row = lambda bi, ci: (bi, ci, 0)
    st = lambda bi, ci: (bi, 0, 0, 0)
    sshape = (nb, GDN_HEADS, GDN_DH, GDN_DH)
    o, s_new = pl.pallas_call(
        functools.partial(_gdn_chunk_kernel, chunk=chunk),
        grid=(b // nb, nc),
        in_specs=[pl.BlockSpec((nb, chunk, d), row)] * 4 + [pl.BlockSpec((nb, chunk, LANES), row),
                                                           pl.BlockSpec(sshape, st),
                                                           pl.BlockSpec((1, GDN_DH), lambda bi, ci: (0, 0))],
        out_specs=[pl.BlockSpec((nb, chunk, d), row), pl.BlockSpec(sshape, st)],
        out_shape=[jax.ShapeDtypeStruct((b, t, d), BF16), jax.ShapeDtypeStruct(s0.shape, F32)],
        compiler_params=_params("parallel", "arbitrary"),
        name="gdn_chunks",
    )(seq(q), seq(k), seq(v), seq(gate), seq(aux), s0, out_gain[None, :])
    return o.reshape(n, d), s_new


def _router_kernel(x_ref, o_ref, wo_ref, g_ref, r_ref, base_ref, x1_ref, h3_ref, info_ref, cnt_ref, cnt_s,
                   *, cap):
    i = pl.program_id(0)
    tm, d = x_ref.shape

    @pl.when(i == 0)
    def _():
        cnt_s[...] = base_ref[...]

    nchunk = d // LANES
    rg = tm // ROUTER_ROW_GROUPS if tm % (ROUTER_ROW_GROUPS * LANES) == 0 else tm
    groups = [slice(r0, r0 + rg) for r0 in range(0, tm, rg)]
    x1s = [x_ref[rs, :] + jnp.dot(o_ref[rs, :], wo_ref[...], preferred_element_type=F32) for rs in groups]
    hs = []
    for rs, x1 in zip(groups, x1s):
        x1_ref[rs, :] = x1
        h = _rms(x1, g_ref[...])
        for c in range(nchunk):
            h3_ref[pl.ds(rs.start * nchunk + c, rg, stride=nchunk), :] = h[:, c * LANES:(c + 1) * LANES]
        hs.append(h)
    logit_l = [jnp.dot(h, r_ref[...], preferred_element_type=F32, precision=HIGHEST) for h in hs]
    lane = lax.broadcasted_iota(jnp.int32, (rg, LANES), 1)
    picks = []
    for logits in logit_l:
        logits = jnp.where(lane < N_EXPERTS, logits, -jnp.inf)
        m1 = jnp.max(logits, axis=1, keepdims=True)
        i1 = jnp.min(jnp.where(logits == m1, lane, LANES), axis=1, keepdims=True)
        rest = jnp.where(lane == i1, -jnp.inf, logits)
        m2 = jnp.max(rest, axis=1, keepdims=True)
        i2 = jnp.min(jnp.where(rest == m2, lane, LANES), axis=1, keepdims=True)
        e2 = jnp.exp(m2 - m1)
        picks.append((i1, i2, 1.0 / (1.0 + e2), e2 / (1.0 + e2), ((lane == i1) | (lane == i2)).astype(F32)))
    ri = lax.broadcasted_iota(jnp.int32, (rg, rg), 0)
    rj = lax.broadcasted_iota(jnp.int32, (rg, rg), 1)
    tri = (rj < ri).astype(BF16)
    ranks = [jnp.dot(tri, p[4].astype(BF16), preferred_element_type=F32) for p in picks]
    base = cnt_s[...]
    for rs, (i1, i2, g1, g2, member), rank in zip(groups, picks, ranks):
        slot = base + rank + lane.astype(F32) * float(cap)
        d1 = jnp.sum(jnp.where(lane == i1, slot, 0.0), axis=1, keepdims=True)
        d2 = jnp.sum(jnp.where(lane == i2, slot, 0.0), axis=1, keepdims=True)
        base = base + jnp.sum(member, axis=0, keepdims=True)
        info = jnp.zeros((rg, LANES), F32)
        for k, val in enumerate((i1.astype(F32), i2.astype(F32), g1, g2, d1, d2)):
            info = jnp.where(lane == k, val, info)
        info_ref[rs, :] = info
    cnt_s[...] = base
    cnt_ref[...] = base


def _moe_router(x, o, w_o, norm_g, router, base_counts, tm, cap):
    n, d = x.shape
    r_pad = jnp.concatenate([router, jnp.zeros((d, LANES - N_EXPERTS), F32)], axis=1)
    row = lambda i: (i, 0)
    const = lambda i: (0, 0)
    x1, h3, info, counts = pl.pallas_call(
        functools.partial(_router_kernel, cap=cap),
        grid=(n // tm,),
        in_specs=[pl.BlockSpec((tm, d), row), pl.BlockSpec((tm, d), row), pl.BlockSpec((d, d), const),
                  pl.BlockSpec((1, d), const), pl.BlockSpec((d, LANES), const), pl.BlockSpec((1, LANES), const)],
        out_specs=[pl.BlockSpec((tm, d), row), pl.BlockSpec((tm * (d // LANES), LANES), row),
                   pl.BlockSpec((tm, LANES), row), pl.BlockSpec((1, LANES), const)],
        out_shape=[jax.ShapeDtypeStruct((n, d), F32), jax.ShapeDtypeStruct((n * (d // LANES), LANES), F32),
                   jax.ShapeDtypeStruct((n, LANES), F32), jax.ShapeDtypeStruct((1, LANES), F32)],
        scratch_shapes=[pltpu.VMEM((1, LANES), F32)],
        compiler_params=_params("arbitrary"),
        name="moe_router",
    )(x, o, w_o.astype(BF16), norm_g[None, :], r_pad, base_counts)
    return x1, h3.reshape(n, d // LANES, LANES), info, counts


def _row_dests(info, tm):
    n = info.shape[0]
    dest = info[:, 4:6].astype(jnp.int32)
    return dest.reshape(n // tm, tm, 2).transpose(0, 2, 1).reshape(n // tm, 1, 2 * tm)


def _dispatch_kernel(*refs, tm, final, tile):
    if final:
        dest_ref, pad_ref, h3_ref, _, xs_ref, sem, zeros_s = refs
    else:
        dest_ref, h3_ref, xs_ref, sem = refs
    i = pl.program_id(0)
    nchunk = xs_ref.shape[1]

    def copies(r):
        src = h3_ref.at[pl.ds(pl.multiple_of(r * nchunk, nchunk), nchunk)]
        return (pltpu.make_async_copy(src, xs_ref.at[dest_ref[0, 0, r]], sem),
                pltpu.make_async_copy(src, xs_ref.at[dest_ref[0, 0, tm + r]], sem))

    def issue(r, carry):
        for prio, cp in enumerate(copies(r)):
            cp.start(priority=prio)
        return carry

    lax.fori_loop(0, tm, issue, 0, unroll=8)
    landed = xs_ref.at[pl.ds(0, 2 * tm)]
    pltpu.make_async_copy(landed, landed, sem).wait()

    if final:
        @pl.when(i == pl.num_programs(0) - 1)
        def _():
            zeros_s[...] = jnp.zeros(zeros_s.shape, F32)
            for e in range(N_EXPERTS):
                start = pad_ref[0, e]
                tail = pad_ref[1, e]
                for bit in reversed(range((tile - 1).bit_length())):
                    size = 1 << bit
                    off = start + ((tail >> (bit + 1)) << (bit + 1))

                    @pl.when(((tail >> bit) & 1) == 1)
                    def _():
                        cp = pltpu.make_async_copy(zeros_s.at[pl.ds(0, size)], xs_ref.at[pl.ds(off, size)], sem)
                        cp.start()
                        cp.wait()


def _moe_dispatch(h3, dest, tm, rows, xs=None, pad=None, tile=None):
    n, s, _ = h3.shape
    final = xs is not None
    smem = functools.partial(pl.BlockSpec, memory_space=pltpu.SMEM)
    anyspec = pl.BlockSpec(memory_space=pl.ANY)
    in_specs = [smem((1, 1, 2 * tm), lambda i: (i, 0, 0))]
    args = [dest]
    if final:
        in_specs.append(smem((2, N_EXPERTS), lambda i: (0, 0)))
        args.append(pad)
    in_specs.append(pl.BlockSpec((tm * s, LANES), lambda i: (i, 0)))
    args.append(h3.reshape(n * s, LANES))
    scratch = [pltpu.SemaphoreType.DMA(())]
    if final:
        in_specs.append(anyspec)
        args.append(xs)
        scratch.append(pltpu.VMEM((1 << ((tile - 1).bit_length() - 1), s, LANES), F32))
    return pl.pallas_call(
        functools.partial(_dispatch_kernel, tm=tm, final=final, tile=tile),
        grid=(n // tm,),
        in_specs=in_specs,
        out_specs=anyspec,
        out_shape=jax.ShapeDtypeStruct((rows, s, LANES), F32),
        scratch_shapes=scratch,
        input_output_aliases={len(args) - 1: 0} if final else {},
        compiler_params=_params("arbitrary"),
        name="moe_dispatch",
    )(*args)


def _experts_kernel(blk_ref, exp_ref, valid_ref, xs_ref, w1_ref, w3_ref, w2_ref, ys_ref, h_s, acc_s):
    t = pl.program_id(0)
    j = pl.program_id(1)
    tile, d = h_s.shape
    nchunk = d // LANES

    @pl.when(valid_ref[t] == 1)
    def _():
        @pl.when(j == 0)
        def _():
            for c in range(nchunk):
                h_s[:, c * LANES:(c + 1) * LANES] = xs_ref[pl.ds(c, tile, stride=nchunk), :].astype(BF16)

        part = _swiglu_step(h_s[...], w1_ref[0], w3_ref[0], w2_ref[0])

        @pl.when(j == 0)
        def _():
            acc_s[...] = part

        @pl.when(j > 0)
        def _():
            acc_s[...] += part

        @pl.when(j == pl.num_programs(1) - 1)
        def _():
            for c in range(nchunk):
                ys_ref[pl.ds(c, tile, stride=nchunk), :] = acc_s[:, c * LANES:(c + 1) * LANES]


def _moe_experts(xs, counts, w1, w3, w2, cap, tile, tf, n_assign):
    rows, s, _ = xs.shape
    ne, d, dff = w1.shape
    nj = dff // tf
    n_tiles = -(-n_assign // tile) + ne
    tiles_e = (counts + tile - 1) // tile
    cum = jnp.cumsum(tiles_e)
    total = cum[-1]
    t = jnp.arange(n_tiles, dtype=jnp.int32)
    t_eff = jnp.maximum(jnp.minimum(t, total - 1), 0)
    e_of_t = jnp.minimum(jnp.sum(t_eff[:, None] >= cum[None, :], axis=1), ne - 1).astype(jnp.int32)
    blk = (e_of_t * (cap // tile) + t_eff - (cum - tiles_e)[e_of_t]).astype(jnp.int32)
    valid = (t < total).astype(jnp.int32)
    jmap = lambda j, v: j * v + (nj - 1) * (1 - v)
    grid_spec = pltpu.PrefetchScalarGridSpec(
        num_scalar_prefetch=3,
        grid=(n_tiles, nj),
        in_specs=[pl.BlockSpec((tile * s, LANES), lambda t, j, blk, ex, va: (blk[t], 0)),
                  pl.BlockSpec((1, d, tf), lambda t, j, blk, ex, va: (ex[t], 0, jmap(j, va[t]))),
                  pl.BlockSpec((1, d, tf), lambda t, j, blk, ex, va: (ex[t], 0, jmap(j, va[t]))),
                  pl.BlockSpec((1, tf, d), lambda t, j, blk, ex, va: (ex[t], jmap(j, va[t]), 0))],
        out_specs=pl.BlockSpec((tile * s, LANES), lambda t, j, blk, ex, va: (blk[t], 0)),
        scratch_shapes=[pltpu.VMEM((tile, d), BF16), pltpu.VMEM((tile, d), F32)])
    ys = pl.pallas_call(
        _experts_kernel,
        grid_spec=grid_spec,
        out_shape=jax.ShapeDtypeStruct((rows * s, LANES), F32),
        compiler_params=_params("arbitrary", "arbitrary"),
        name="moe_experts",
    )(blk, e_of_t, valid, xs.reshape(rows * s, LANES), w1, w3, w2)
    return ys.reshape(rows, s, LANES)


def _combine_kernel(dest_ref, x1_ref, info_ref, ys_ref, y_ref, buf, sem):
    tm, d = x1_ref.shape
    nchunk = d // LANES

    def slab(k):
        return buf.at[pl.ds(pl.multiple_of(k * nchunk, nchunk), nchunk)]

    def copies(r):
        return (pltpu.make_async_copy(ys_ref.at[dest_ref[0, 0, r]], slab(r), sem),
                pltpu.make_async_copy(ys_ref.at[dest_ref[0, 0, tm + r]], slab(tm + r), sem))

    def issue(r, carry):
        for prio, cp in enumerate(copies(r)):
            cp.start(priority=prio)
        return carry

    lax.fori_loop(0, tm, issue, 0, unroll=8)
    pltpu.make_async_copy(buf, buf, sem).wait()
    g1 = info_ref[:, 2:3]
    g2 = info_ref[:, 3:4]
    for c in range(nchunk):
        cs = slice(c * LANES, (c + 1) * LANES)
        r1 = buf[pl.ds(c, tm, stride=nchunk), :]
        r2 = buf[pl.ds(tm * nchunk + c, tm, stride=nchunk), :]
        y_ref[:, cs] = x1_ref[:, cs] + g1 * r1 + g2 * r2


def _moe_combine(x1, info, dest, ys, tm):
    n, d = x1.shape
    s = d // LANES
    row = lambda i: (i, 0)
    return pl.pallas_call(
        _combine_kernel,
        grid=(n // tm,),
        in_specs=[pl.BlockSpec((1, 1, 2 * tm), lambda i: (i, 0, 0), memory_space=pltpu.SMEM),
                  pl.BlockSpec((tm, d), row), pl.BlockSpec((tm, LANES), row),
                  pl.BlockSpec(memory_space=pl.ANY)],
        out_specs=pl.BlockSpec((tm, d), row),
        out_shape=jax.ShapeDtypeStruct((n, d), F32),
        scratch_shapes=[pltpu.VMEM((2 * tm * s, LANES), F32), pltpu.SemaphoreType.DMA(())],
        compiler_params=_params("arbitrary"),
        name="moe_combine",
    )(dest, x1, info, ys)


def _moe_block(streams, moe_w, w_o):
    moe_norm, router, w1, w3, w2 = moe_w
    n_total = sum(x.shape[0] for x, _ in streams)
    share = -(-2 * n_total * 11 // (10 * N_EXPERTS))
    tile = -(-share // (MOE_TILES_PER_EXPERT * LANES)) * LANES
    cap = -(-n_total // tile) * tile
    rows = N_EXPERTS * cap
    counts = jnp.zeros((1, LANES), F32)
    routed = []
    for x, o in streams:
        tm = min(512, x.shape[0])
        x1, h3, info, counts = _moe_router(x, o, w_o, moe_norm, router, counts, tm, cap)
        routed.append((x1, h3, info, _row_dests(info, tm), tm))
    cnt = counts[0, :N_EXPERTS].astype(jnp.int32)
    pad = jnp.stack([jnp.arange(N_EXPERTS, dtype=jnp.int32) * cap + cnt, (-cnt) % tile])
    xs = None
    for k, (x1, h3, info, dest, tm) in enumerate(routed):
        if k + 1 < len(routed):
            assert xs is None, "only the first and the final dispatch are distinguished"
            xs = _moe_dispatch(h3, dest, tm, rows)
        else:
            xs = _moe_dispatch(h3, dest, tm, rows, xs=xs, pad=pad, tile=tile)
    ys = _moe_experts(xs, cnt, w1, w3, w2, cap, tile, MOE_TF, 2 * n_total)
    return [_moe_combine(x1, info, dest, ys, tm) for x1, h3, info, dest, tm in routed]


def _fox_layer(x, b, t, cache, w, ffn_w):
    n, d = x.shape
    tm = min(512, n)
    q, k, v, logf = _fox_proj(x, *w[:6], tm=tm)
    if cache is None:
        c = _cumsum_time(logf.reshape(b, t, FOX_HEADS))
        o = _fox_attn_prompt(q, k, v, c, b, t, tq=min(512, t), tk=min(512, t))
    else:
        ck, cv, cl = cache
        past = ck.shape[1]
        c = _cumsum_time(jnp.concatenate([cl, logf.reshape(b, t, FOX_HEADS)], axis=1))
        o = _fox_attn_sample(q, k, v, ck.reshape(b, past, d), cv.reshape(b, past, d), c, b, t)
    y = _attn_out_ffn(x, o, w[6], *ffn_w, tm=min(1024, n), tf=512)
    return y, k, v, logf


def _gdn_mixer(x, b, t, chunk, hist, s0, w):
    norm_g, w_qkv, conv_w, w_beta, w_a, a_log, dt_bias, w_gate, out_gain, _ = w
    tm = min(512, t)
    q, k, v, gate, aux, new_hist = _gdn_proj(x, hist, b, t, tm, norm_g, w_qkv, conv_w, w_beta, w_a,
                                             a_log, dt_bias, w_gate)
    groups_per_seq = GDN_HEADS // (LANES // chunk)
    nb = max(1, min(b, GDN_CHAINS_PER_STEP // groups_per_seq))
    o, s_new = _gdn_chunks(q, k, v, gate, aux, s0, out_gain, b, t, chunk, nb)
    return o, s_new, new_hist


def kernel(x_prompt, x_sample, cache_fox_k, cache_fox_v, cache_fox_logf, state_gdn_s, state_gdn_conv, fox_norm, fox_w_qkv, fox_q_gain, fox_k_gain, fox_w_f, fox_b_f, fox_w_o, ffn_norm, ffn_w1, ffn_w3, ffn_w2, gdn_norm, gdn_w_qkv, gdn_conv_w, gdn_w_beta, gdn_w_a, gdn_a_log, gdn_dt_bias, gdn_w_gate, gdn_out_gain, gdn_w_o, moe_norm, moe_router, moe_w1, moe_w3, moe_w2):
    bp, tp, d = x_prompt.shape
    bs, ts, _ = x_sample.shape
    depth = fox_norm.shape[0] + gdn_norm.shape[0]
    cdim = gdn_w_qkv.shape[2]
    yp = x_prompt.reshape(bp * tp, d)
    ys = x_sample.reshape(bs * ts, d)
    kp_l, vp_l, lp_l, ks_l, vs_l, ls_l = [], [], [], [], [], []
    sp_l, cp_l, ss_l, cs_l = [], [], [], []
    for i in range(depth):
        j = i // 2
        if i % 2 == 0:
            fw = (fox_norm[j], fox_w_qkv[j], fox_q_gain[j], fox_k_gain[j], fox_w_f[j], fox_b_f[j], fox_w_o[j])
            ffw = (ffn_norm[j], ffn_w1[j], ffn_w3[j], ffn_w2[j])
            yp, kp, vp, lp = _fox_layer(yp, bp, tp, None, fw, ffw)
            ys, ks, vs, ls = _fox_layer(ys, bs, ts, (cache_fox_k[j], cache_fox_v[j], cache_fox_logf[j]), fw, ffw)
            kp_l.append(kp.reshape(bp, tp, FOX_HEADS, FOX_DH)); vp_l.append(vp.reshape(bp, tp, FOX_HEADS, FOX_DH))
            lp_l.append(lp.reshape(bp, tp, FOX_HEADS))
            ks_l.append(ks.reshape(bs, ts, FOX_HEADS, FOX_DH)); vs_l.append(vs.reshape(bs, ts, FOX_HEADS, FOX_DH))
            ls_l.append(ls.reshape(bs, ts, FOX_HEADS))
        else:
            gw = (gdn_norm[j], gdn_w_qkv[j], gdn_conv_w[j], gdn_w_beta[j], gdn_w_a[j], gdn_a_log[j],
                  gdn_dt_bias[j], gdn_w_gate[j], gdn_out_gain[j], gdn_w_o[j])
            mw = (moe_norm[j], moe_router[j], moe_w1[j], moe_w3[j], moe_w2[j])
            hist0 = jnp.zeros((bp, GDN_CONV - 1, cdim), F32)
            s_zero = jnp.zeros((bp, GDN_HEADS, GDN_DH, GDN_DH), F32)
            op, sp, cp = _gdn_mixer(yp, bp, tp, 64, hist0, s_zero, gw)
            os_, ss, cs = _gdn_mixer(ys, bs, ts, ts, state_gdn_conv[j], state_gdn_s[j], gw)
            yp, ys = _moe_block([(yp, op), (ys, os_)], mw, gw[-1])
            sp_l.append(sp); cp_l.append(cp); ss_l.append(ss); cs_l.append(cs)
    return (yp.reshape(bp, tp, d), ys.reshape(bs, ts, d),
            jnp.stack(kp_l), jnp.stack(vp_l), jnp.stack(lp_l),
            jnp.stack(ks_l), jnp.stack(vs_l), jnp.stack(ls_l),
            jnp.stack(sp_l), jnp.stack(cp_l), jnp.stack(ss_l), jnp.stack(cs_l))
```
